```python
import math
import jax
import jax.numpy as jnp
from jax import lax
import numpy as np

D_MODEL = 1024
BATCH = 4
SEQ = 4096
DEPTH = 2
DEC_BATCH = 128
DEC_SEQ = 1
PAST_LEN = 2048
PAGE_SIZE = 128

D_MIX = D_MODEL
HEAD_DIM = 64
D_A = D_MIX // 2
H_A = D_A // HEAD_DIM
DILATED_PAIRS = ((128, 1), (512, 4), (2048, 16))
W_MAX = 2048
BLK = 128
ATTN_SCALE = HEAD_DIM ** -0.5
N_BUCKETS = 32
MAX_DIST = 2048
D_B = D_MIX // 4
G_B = 4
E_B = D_B // G_B
CHUNK = 128
D_C = D_MIX // 4
H_C = 4
E_C = D_C // H_C
CONV_W = 4
LRU_C = 8.0
PEER_HEADS = 8
N_KEYS = 128
N_EXPERTS = N_KEYS * N_KEYS
PEER_TOPK = 16
D_QUERY = 256
PEER_BLOCK = 128
D_IN = 3 * D_A + 2 * D_B + 2 * D_C
SPLITS = (D_A, 2 * D_A, 3 * D_A, 3 * D_A + D_B, 3 * D_A + 2 * D_B, 3 * D_A + 2 * D_B + D_C)
EPS = 1e-6
NEG = -1e30

kernel_name = 'hybrid_dilated_gmlp_rglru_peer_step'


def rmsnorm(x, g=None):
    xf = x.astype(jnp.float32)
    y = xf * lax.rsqrt(jnp.mean(xf * xf, axis=-1, keepdims=True) + EPS)
    if g is not None:
        y = y * g.astype(jnp.float32)
    return y.astype(x.dtype)


def t5_bucket(dist):
    exact = N_BUCKETS // 2
    df = jnp.maximum(dist, 1).astype(jnp.float32)
    large = exact + (jnp.log(df / exact) / math.log(MAX_DIST / exact) * (N_BUCKETS - exact)).astype(jnp.int32)
    return jnp.where(dist < exact, dist, jnp.minimum(large, N_BUCKETS - 1))


def _phase_blocks(x, dil, nb):
    b, s, h, e = x.shape
    L = s // dil
    xp = x.reshape(b, L, dil, h, e).transpose(0, 2, 1, 3, 4)
    xp = jnp.pad(xp, ((0, 0), (0, 0), (0, nb * BLK - L), (0, 0), (0, 0)))
    return xp.reshape(b, dil, nb, BLK, h, e)


def _with_prev(xb):
    prev = jnp.pad(xb[:, :, :-1], ((0, 0), (0, 0), (1, 0), (0, 0), (0, 0), (0, 0)))
    return jnp.concatenate([prev, xb], axis=3)


def _merge_branches(outs, lses):
    wts = jax.nn.softmax(jnp.stack(lses), axis=0)
    return jnp.einsum('rbth,rbthe->bthe', wts, jnp.stack(outs))


def dilated_attn_prompt(q, k, v, rel_bias):
    b, s, h, e = q.shape
    qf, kf, vf = q.astype(jnp.float32), k.astype(jnp.float32), v.astype(jnp.float32)
    outs, lses = [], []
    for window, dil in DILATED_PAIRS:
        L = s // dil
        nb = -(-L // BLK)
        qb = _phase_blocks(qf, dil, nb)
        kc = _with_prev(_phase_blocks(kf, dil, nb))
        vc = _with_prev(_phase_blocks(vf, dil, nb))
        qi = jnp.arange(BLK)[:, None] + BLK
        ki = jnp.arange(2 * BLK)[None, :]
        dm = qi - ki
        m_k = jnp.arange(nb)[:, None, None] * BLK - BLK + ki[None]
        valid = (dm >= 0) & (dm <= window // dil) & (m_k >= 0)
        bias = jnp.transpose(rel_bias[t5_bucket(jnp.maximum(dm, 0) * dil)], (2, 0, 1)).astype(jnp.float32)
        logits = jnp.einsum('bpnqhe,bpnkhe->bpnhqk', qb, kc) * ATTN_SCALE + bias
        logits = jnp.where(valid[:, None], logits, NEG)
        lse = jax.nn.logsumexp(logits, axis=-1)
        p = jnp.exp(logits - lse[..., None])
        o = jnp.einsum('bpnhqk,bpnkhe->bpnqhe', p, vc)
        o = o.reshape(b, dil, nb * BLK, h, e)[:, :, :L].transpose(0, 2, 1, 3, 4).reshape(b, s, h, e)
        lse = lse.transpose(0, 1, 2, 4, 3).reshape(b, dil, nb * BLK, h)[:, :, :L]
        lse = lse.transpose(0, 2, 1, 3).reshape(b, s, h)
        outs.append(o)
        lses.append(lse)
    return _merge_branches(outs, lses)


def dilated_attn_sample(q, k_all, v_all, rel_bias):
    b, t, h, e = q.shape
    w_c = k_all.shape[1] - t
    qf = q.astype(jnp.float32)
    outs, lses = [], []
    for window, dil in DILATED_PAIRS:
        j = jnp.arange(window // dil + 1)
        idx = (w_c + jnp.arange(t))[:, None] - j[None, :] * dil
        valid = idx >= 0
        idx = jnp.maximum(idx, 0)
        kg = k_all[:, idx].astype(jnp.float32)
        vg = v_all[:, idx].astype(jnp.float32)
        bias = rel_bias[t5_bucket(j * dil)].T.astype(jnp.float32)
        logits = jnp.einsum('bthe,btjhe->bthj', qf, kg) * ATTN_SCALE + bias
        logits = jnp.where(valid[:, None, :], logits, NEG)
        lse = jax.nn.logsumexp(logits, axis=-1)
        p = jnp.exp(logits - lse[..., None])
        outs.append(jnp.einsum('bthj,btjhe->bthe', p, vg))
        lses.append(lse)
    return _merge_branches(outs, lses)


def spatial_gate(u, v, w_s, b_s):
    n = v.shape[2]
    w = jnp.tril(w_s[:, :n, :n])
    mix = jnp.einsum('gij,bcjge->bcige', w, v) + b_s[:, :n].T[None, None, :, :, None]
    return u * mix


def linear_scan(a, bterm, h0):
    bterm = bterm.at[:, 0].add(a[:, 0] * h0)

    def combine(left, right):
        a_l, b_l = left
        a_r, b_r = right
        return a_l * a_r, a_r * b_l + b_r

    _, h = lax.associative_scan(combine, (a, bterm), axis=1)
    return h


def rglru_block(xr, xg, conv_buf, h0, conv_w, conv_b, w_a, b_a, w_x, b_x, lam):
    b, t, _ = xr.shape
    xp = jnp.concatenate([conv_buf.astype(xr.dtype), xr], axis=1)
    xc = conv_b + sum(xp[:, k:k + t] * conv_w[k] for k in range(CONV_W))
    new_buf = xp[:, t:]
    xh = xc.reshape(b, t, H_C, E_C)
    r = jax.nn.sigmoid((jnp.einsum('bthi,hij->bthj', xh, w_a).reshape(b, t, D_C) + b_a).astype(jnp.float32))
    ig = jax.nn.sigmoid((jnp.einsum('bthi,hij->bthj', xh, w_x).reshape(b, t, D_C) + b_x).astype(jnp.float32))
    log_a = -LRU_C * r * jax.nn.softplus(-lam.astype(jnp.float32))
    a = jnp.exp(log_a)
    bterm = jnp.sqrt(-jnp.expm1(2.0 * log_a)) * ig * xc.astype(jnp.float32)
    h = linear_scan(a, bterm, h0.astype(jnp.float32))
    y = jax.nn.gelu(xg.astype(jnp.float32)) * h
    return y.astype(xr.dtype), new_buf, h[:, -1].astype(xr.dtype)


def peer_ffn(h, w_q, sub_keys, u_tab, v_tab):
    b, t, d = h.shape
    n = b * t
    x = h.reshape(n, d)
    q = rmsnorm((x @ w_q).reshape(n, PEER_HEADS, 2, D_QUERY // 2)).astype(jnp.float32)
    s = jnp.einsum('nhpe,hpke->nhpk', q, sub_keys.astype(jnp.float32))
    v1, i1 = lax.top_k(s[:, :, 0], PEER_TOPK)
    v2, i2 = lax.top_k(s[:, :, 1], PEER_TOPK)
    cand = (v1[..., :, None] + v2[..., None, :]).reshape(n, PEER_HEADS, PEER_TOPK * PEER_TOPK)
    sv, ci = lax.top_k(cand, PEER_TOPK)
    e_idx = (jnp.take_along_axis(i1, ci // PEER_TOPK, axis=-1) * N_KEYS
             + jnp.take_along_axis(i2, ci % PEER_TOPK, axis=-1))
    g = jax.nn.softmax(sv, axis=-1)
    nk = PEER_HEADS * PEER_TOPK
    nblk = -(-n // PEER_BLOCK)
    pad = nblk * PEER_BLOCK - n
    xb = jnp.pad(x, ((0, pad), (0, 0))).reshape(nblk, PEER_BLOCK, d)
    eb = jnp.pad(e_idx.reshape(n, nk), ((0, pad), (0, 0))).reshape(nblk, PEER_BLOCK, nk)
    gb = jnp.pad(g.reshape(n, nk), ((0, pad), (0, 0))).reshape(nblk, PEER_BLOCK, nk)

    def block(args):
        xx, ee, gg = args
        act = jax.nn.gelu(jnp.einsum('nd,nkd->nk', xx.astype(jnp.float32), u_tab[ee].astype(jnp.float32)))
        return jnp.einsum('nk,nkd->nd', gg * act, v_tab[ee].astype(jnp.float32))

    out = lax.map(block, (xb, eb, gb)).reshape(nblk * PEER_BLOCK, d)[:n]
    return out.reshape(b, t, d).astype(h.dtype)


def trunk_layer(x, c, lp, rel_bias, cache_k, cache_v, conv_buf, h0):
    b, t, _ = x.shape
    sh1, sc1, g1, sh2, sc2, g2 = jnp.split((jax.nn.silu(c) @ lp['w_ada'] + lp['b_ada'])[:, None, :], 6, axis=-1)
    h = rmsnorm(x, lp['norm_mix']) * (1.0 + sc1) + sh1
    qa, ka, va, ub, vb, xr, xg = jnp.split(h @ lp['w_in'], SPLITS, axis=-1)
    q = rmsnorm(qa.reshape(b, t, H_A, HEAD_DIM), lp['q_gain'])
    k = rmsnorm(ka.reshape(b, t, H_A, HEAD_DIM), lp['k_gain'])
    v = va.reshape(b, t, H_A, HEAD_DIM)
    vb = rmsnorm(vb, lp['gmlp_norm'])
    if cache_k is None:
        ya = dilated_attn_prompt(q, k, v, rel_bias)
        keep = min(W_MAX, t)
        k_rows, v_rows = k[:, t - keep:], v[:, t - keep:]
        nc, n = t // CHUNK, CHUNK
        gv_rows = vb[:, t - CHUNK:]
    else:
        k_all = jnp.concatenate([cache_k.astype(k.dtype), k], axis=1)
        v_all = jnp.concatenate([cache_v.astype(v.dtype), v], axis=1)
        ya = dilated_attn_sample(q, k_all, v_all, rel_bias)
        k_rows, v_rows = k, v
        nc, n = 1, t
        gv_rows = vb
    yb = spatial_gate(ub.reshape(b, nc, n, G_B, E_B), vb.reshape(b, nc, n, G_B, E_B),
                      lp['w_s'], lp['b_s']).reshape(b, t, D_B)
    yc, new_buf, h_last = rglru_block(xr, xg, conv_buf, h0, lp['conv_w'], lp['conv_b'],
                                      lp['w_a'], lp['b_a'], lp['w_x'], lp['b_x'], lp['lru_lambda'])
    og = lp['out_gain']
    mix = jnp.concatenate([rmsnorm(ya.reshape(b, t, D_A).astype(x.dtype), og[:D_A]),
                           rmsnorm(yb, og[D_A:D_A + D_B]),
                           rmsnorm(yc, og[D_A + D_B:])], axis=-1)
    x = x + g1 * (mix @ lp['w_out'])
    h2 = rmsnorm(x, lp['norm_ffn']) * (1.0 + sc2) + sh2
    x = x + g2 * peer_ffn(h2, lp['peer_wq'], lp['peer_keys'], lp['peer_u'], lp['peer_v'])
    return x, (k_rows, v_rows, gv_rows, new_buf, h_last)


def setup_inputs(seed: int = 0) -> dict:
    key = jax.random.key(seed)
    ks = iter(jax.random.split(key, 40))

    def nrm(shape, scale):
        return scale * jax.random.normal(next(ks), shape, jnp.float32)

    w_cache = min(W_MAX, PAST_LEN)
    a0 = jax.random.uniform(next(ks), (DEPTH, D_C), jnp.float32, minval=0.9, maxval=0.999)
    return {
        'x_prompt': nrm((BATCH, SEQ, D_MODEL), 1.0),
        'x_sample': nrm((DEC_BATCH, DEC_SEQ, D_MODEL), 1.0),
        'cache_k': nrm((DEPTH, DEC_BATCH, w_cache, H_A, HEAD_DIM), 1.0),
        'cache_v': nrm((DEPTH, DEC_BATCH, w_cache, H_A, HEAD_DIM), 1.0),
        'state_conv': nrm((DEPTH, DEC_BATCH, CONV_W - 1, D_C), 1.0),
        'state_h': nrm((DEPTH, DEC_BATCH, D_C), 0.5),
        'c_prompt': nrm((BATCH, D_MODEL), 1.0),
        'c_sample': nrm((DEC_BATCH, D_MODEL), 1.0),
        'rel_bias': nrm((N_BUCKETS, H_A), 0.1),
        'w_ada': nrm((DEPTH, D_MODEL, 6 * D_MODEL), 0.5 * D_MODEL ** -0.5),
        'b_ada': nrm((DEPTH, 6 * D_MODEL), 0.01),
        'norm_mix': 1.0 + nrm((DEPTH, D_MODEL), 0.05),
        'norm_ffn': 1.0 + nrm((DEPTH, D_MODEL), 0.05),
        'w_in': nrm((DEPTH, D_MODEL, D_IN), D_MODEL ** -0.5),
        'q_gain': 1.0 + nrm((DEPTH, HEAD_DIM), 0.05),
        'k_gain': 1.0 + nrm((DEPTH, HEAD_DIM), 0.05),
        'gmlp_norm': 1.0 + nrm((DEPTH, D_B), 0.05),
        'w_s': nrm((DEPTH, G_B, CHUNK, CHUNK), CHUNK ** -0.5),
        'b_s': 1.0 + nrm((DEPTH, G_B, CHUNK), 0.05),
        'conv_w': nrm((DEPTH, CONV_W, D_C), CONV_W ** -0.5),
        'conv_b': nrm((DEPTH, D_C), 0.01),
        'w_a': nrm((DEPTH, H_C, E_C, E_C), E_C ** -0.5),
        'b_a': nrm((DEPTH, D_C), 0.01),
        'w_x': nrm((DEPTH, H_C, E_C, E_C), E_C ** -0.5),
        'b_x': nrm((DEPTH, D_C), 0.01),
        'lru_lambda': jnp.log(a0) - jnp.log1p(-a0),
        'out_gain': 1.0 + nrm((DEPTH, D_MIX), 0.05),
        'w_out': nrm((DEPTH, D_MIX, D_MODEL), D_MIX ** -0.5),
        'peer_wq': nrm((DEPTH, D_MODEL, PEER_HEADS * D_QUERY), D_MODEL ** -0.5),
        'peer_keys': nrm((DEPTH, PEER_HEADS, 2, N_KEYS, D_QUERY // 2), (D_QUERY // 2) ** -0.5),
        'peer_u': nrm((DEPTH, N_EXPERTS, D_MODEL), D_MODEL ** -0.5),
        'peer_v': nrm((DEPTH, N_EXPERTS, D_MODEL), (PEER_HEADS * PEER_TOPK) ** -0.5),
    }


def reference(x_prompt, x_sample, cache_k, cache_v, state_conv, state_h, c_prompt, c_sample,
              rel_bias, w_ada, b_ada, norm_mix, norm_ffn, w_in, q_gain, k_gain, gmlp_norm,
              w_s, b_s, conv_w, conv_b, w_a, b_a, w_x, b_x, lru_lambda, out_gain, w_out,
              peer_wq, peer_keys, peer_u, peer_v):
    xp, xs = x_prompt, x_sample
    zeros_buf = jnp.zeros((x_prompt.shape[0], CONV_W - 1, D_C), x_prompt.dtype)
    zeros_h = jnp.zeros((x_prompt.shape[0], D_C), x_prompt.dtype)
    st_p, st_s = [], []
    for l in range(DEPTH):
        lp = dict(w_ada=w_ada[l], b_ada=b_ada[l], norm_mix=norm_mix[l], norm_ffn=norm_ffn[l],
                  w_in=w_in[l], q_gain=q_gain[l], k_gain=k_gain[l], gmlp_norm=gmlp_norm[l],
                  w_s=w_s[l], b_s=b_s[l], conv_w=conv_w[l], conv_b=conv_b[l], w_a=w_a[l],
                  b_a=b_a[l], w_x=w_x[l], b_x=b_x[l], lru_lambda=lru_lambda[l],
                  out_gain=out_gain[l], w_out=w_out[l], peer_wq=peer_wq[l],
                  peer_keys=peer_keys[l], peer_u=peer_u[l], peer_v=peer_v[l])
        xp, sp = trunk_layer(xp, c_prompt, lp, rel_bias, None, None, zeros_buf, zeros_h)
        xs, ss = trunk_layer(xs, c_sample, lp, rel_bias, cache_k[l], cache_v[l], state_conv[l], state_h[l])
        st_p.append(sp)
        st_s.append(ss)
    k_prompt = jnp.stack([s[0] for s in st_p])
    v_prompt = jnp.stack([s[1] for s in st_p])
    gv_prompt = jnp.stack([s[2] for s in st_p])
    conv_prompt = jnp.stack([s[3] for s in st_p])
    h_prompt = jnp.stack([s[4] for s in st_p])
    k_sample = jnp.stack([s[0] for s in st_s])
    v_sample = jnp.stack([s[1] for s in st_s])
    gv_sample = jnp.stack([s[2] for s in st_s])
    conv_sample = jnp.stack([s[3] for s in st_s])
    h_sample = jnp.stack([s[4] for s in st_s])
    return (xp, xs, k_prompt, v_prompt, k_sample, v_sample, gv_prompt, gv_sample,
            conv_prompt, conv_sample, h_prompt, h_sample)
```

```python
import functools
import math

import jax
import jax.numpy as jnp
import numpy as np
from jax import lax
from jax.experimental import pallas as pl
from jax.experimental.pallas import tpu as pltpu

F32 = jnp.float32
BF16 = jnp.bfloat16
HIGHEST = lax.Precision.HIGHEST

D_MODEL = 1024
HEAD_DIM = 64
D_A = 512
H_A = 8
DILATED_PAIRS = ((128, 1), (512, 4), (2048, 16))
W_MAX = 2048
BLK = 128
ATTN_SCALE = HEAD_DIM ** -0.5
N_BUCKETS = 32
MAX_DIST = 2048
D_B = 256
G_B = 4
E_B = 64
CHUNK = 128
D_C = 256
H_C = 4
E_C = 64
CONV_W = 4
LRU_C = 8.0
PEER_HEADS = 8
N_KEYS = 128
PEER_TOPK = 16
D_QUERY = 256
D_IN = 3 * D_A + 2 * D_B + 2 * D_C
EPS = 1e-6
NEG = -1e30
BIG = 3.0e38

VMEM_LIMIT_BYTES = 56 * 1024 * 1024

TM_PROJ = 512
TT_LRU = 512
TN_SELECT = 256
TN_DENSE = 512
TE_DENSE = 512
BS_ATTN_SAMPLE = 8


def _params(*sem):
    return pltpu.CompilerParams(dimension_semantics=sem, vmem_limit_bytes=VMEM_LIMIT_BYTES)


def _split_dot(x, w_bf16):
    hi = x.astype(BF16)
    lo = (x - hi.astype(F32)).astype(BF16)
    return (jnp.dot(hi, w_bf16, preferred_element_type=F32)
            + jnp.dot(lo, w_bf16, preferred_element_type=F32))


def _rms_rows(x):
    return x * lax.rsqrt(jnp.mean(x * x, axis=-1, keepdims=True) + EPS)


def _ada_kernel(c_ref, w_ref, b_ref, o_ref):
    c = c_ref[...]
    o_ref[...] = jnp.dot(jax.nn.silu(c), w_ref[...], precision=HIGHEST,
                         preferred_element_type=F32) + b_ref[...]


def _ada(c_all, w_ada, b_ada):
    depth, _, n_out = w_ada.shape
    rows = c_all.shape[0]
    tn = 512
    return pl.pallas_call(
        _ada_kernel,
        grid=(depth, n_out // tn),
        in_specs=[pl.BlockSpec((rows, D_MODEL), lambda l, j: (0, 0)),
                  pl.BlockSpec((None, D_MODEL, tn), lambda l, j: (l, 0, j)),
                  pl.BlockSpec((None, 1, tn), lambda l, j: (l, 0, j))],
        out_specs=pl.BlockSpec((None, rows, tn), lambda l, j: (l, 0, j)),
        out_shape=jax.ShapeDtypeStruct((depth, rows, n_out), F32),
        compiler_params=_params("parallel", "parallel"),
        name="ada",
    )(c_all, w_ada, b_ada.reshape(depth, 1, n_out))


def _inproj_kernel(x_ref, sh_ref, sc_ref, nm_ref, w_ref, qg_ref, kg_ref, gn_ref, ones_ref,
                   gw_ref, gb_ref, q_ref, k_ref, v_ref, yb_ref, gv_ref, xr_ref, xg_ref, *, chunked):
    x = x_ref[...]
    h = _rms_rows(x) * nm_ref[...] * (1.0 + sc_ref[...]) + sh_ref[...]
    z = jnp.dot(h.astype(BF16), w_ref[...], preferred_element_type=F32)
    ones = ones_ref[...]

    def headnorm(a, g):
        ss = _split_dot(a * a, ones)
        return a * lax.rsqrt(ss * (1.0 / HEAD_DIM) + EPS) * g

    q_ref[...] = headnorm(z[:, 0:D_A], qg_ref[...]) * ATTN_SCALE
    k_ref[...] = headnorm(z[:, D_A:2 * D_A], kg_ref[...])
    v_ref[...] = z[:, 2 * D_A:3 * D_A]
    o = 3 * D_A
    ub = z[:, o:o + D_B]
    vbn = _rms_rows(z[:, o + D_B:o + 2 * D_B]) * gn_ref[...]
    gv_ref[...] = vbn
    xr_ref[...] = z[:, o + 2 * D_B:o + 2 * D_B + D_C]
    xg_ref[...] = z[:, o + 2 * D_B + D_C:o + 2 * D_B + 2 * D_C]
    if chunked:
        rg = lax.broadcasted_iota(jnp.int32, (G_B * CHUNK, D_B), 0) // CHUNK
        cg = lax.broadcasted_iota(jnp.int32, (G_B * CHUNK, D_B), 1) // E_B
        keep = rg == cg
        for ci in range(x.shape[0] // CHUNK):
            rows = slice(ci * CHUNK, (ci + 1) * CHUNK)
            vc = vbn[rows]
            vbd = jnp.where(keep, jnp.concatenate([vc] * G_B, axis=0), 0.0)
            mix = _split_dot_rhs(gw_ref[...], vbd) + gb_ref[...]
            yb_ref[rows, :] = ub[rows] * mix
    else:
        yb_ref[...] = ub * (gw_ref[...] * vbn + gb_ref[...])


def _split_dot_rhs(w_bf16_pair, x):
    w_hi, w_lo = w_bf16_pair[0], w_bf16_pair[1]
    x_hi = x.astype(BF16)
    x_lo = (x - x_hi.astype(F32)).astype(BF16)
    return (jnp.dot(w_hi, x_hi, preferred_element_type=F32)
            + jnp.dot(w_hi, x_lo, preferred_element_type=F32)
            + jnp.dot(w_lo, x_hi, preferred_element_type=F32))


def _bf16_pair(w):
    hi = w.astype(BF16)
    lo = (w - hi.astype(F32)).astype(BF16)
    return jnp.stack([hi, lo])


def _inproj(x, sh, sc, lp, *, chunked):
    g, t, _ = x.shape
    tm = min(TM_PROJ, t)
    r = sh.shape[1]
    mod_spec = pl.BlockSpec((None, r, D_MODEL), (lambda b, i: (b, 0, 0)))
    full = lambda a: pl.BlockSpec(a.shape, lambda b, i: (0,) * a.ndim)
    if chunked:
        gw = _bf16_pair(jnp.transpose(jnp.tril(lp['w_s']), (1, 0, 2)).reshape(CHUNK, G_B * CHUNK))
        gb = jnp.repeat(lp['b_s'].T, E_B, axis=1)
    else:
        gw = jnp.repeat(lp['w_s'][:, 0, 0], E_B)[None, :]
        gb = jnp.repeat(lp['b_s'][:, 0], E_B)[None, :]
    weights = [lp['norm_mix'][None, :], lp['w_in'].astype(BF16),
               jnp.tile(lp['q_gain'], H_A)[None, :], jnp.tile(lp['k_gain'], H_A)[None, :],
               lp['gmlp_norm'][None, :], _head_ones(), gw, gb]
    row = lambda d: pl.BlockSpec((None, tm, d), lambda b, i: (b, i, 0))
    widths = (D_A, D_A, D_A, D_B, D_B, D_C, D_C)
    return pl.pallas_call(
        functools.partial(_inproj_kernel, chunked=chunked),
        grid=(g, t // tm),
        in_specs=[row(D_MODEL), mod_spec, mod_spec] + [full(w) for w in weights],
        out_specs=[row(d) for d in widths],
        out_shape=[jax.ShapeDtypeStruct((g, t, d), F32) for d in widths],
        compiler_params=_params("parallel", "parallel"),
        name="inproj_prompt" if chunked else "inproj_sample",
    )(x, sh, sc, *weights)


def _head_ones():
    hid = np.arange(D_A) // HEAD_DIM
    return jnp.asarray((hid[:, None] == hid[None, :]).astype(np.float32), dtype=BF16)


def _t5_bucket(dist):
    exact = N_BUCKETS // 2
    df = jnp.maximum(dist, 1).astype(jnp.float32)
    large = exact + (jnp.log(df / exact) / math.log(MAX_DIST / exact) * (N_BUCKETS - exact)).astype(jnp.int32)
    return jnp.where(dist < exact, dist, jnp.minimum(large, N_BUCKETS - 1))


def _prompt_bias(rel_bias, window, dil):
    qi = jnp.arange(BLK)[:, None] + BLK
    ki = jnp.arange(2 * BLK)[None, :]
    dm = qi - ki
    valid = (dm >= 0) & (dm <= window // dil)
    bias = jnp.transpose(rel_bias[_t5_bucket(jnp.maximum(dm, 0) * dil)], (2, 0, 1)).astype(F32)
    return jnp.where(valid[None], bias, NEG)


def _attn_prompt_kernel(q_ref, kp_ref, kc_ref, vp_ref, vc_ref, bias_ref, o_ref, lse_ref):
    first = pl.program_id(2) == 0
    q = q_ref[...].astype(BF16)
    k2 = jnp.concatenate([kp_ref[...], kc_ref[...]], axis=0).astype(BF16)
    v2 = jnp.concatenate([vp_ref[...], vc_ref[...]], axis=0).astype(BF16)
    col = lax.broadcasted_iota(jnp.int32, (BLK, 2 * BLK), 1)
    no_prev = jnp.logical_and(first, col < BLK)
    for h in range(H_A):
        hs = slice(h * HEAD_DIM, (h + 1) * HEAD_DIM)
        lg = lax.dot_general(q[:, hs], k2[:, hs], (((1,), (1,)), ((), ())),
                             preferred_element_type=F32) + bias_ref[h]
        lg = jnp.where(no_prev, NEG, lg)
        m = jnp.max(lg, axis=-1, keepdims=True)
        p = jnp.exp(lg - m)
        s = jnp.sum(p, axis=-1, keepdims=True)
        o = jnp.dot(p.astype(BF16), v2[:, hs], preferred_element_type=F32) / s
        o_ref[:, hs] = o
        lse_ref[:, hs] = jnp.broadcast_to(m + jnp.log(s), (BLK, HEAD_DIM))


def _attn_prompt(q, k, v, bias, dil):
    b, s, _ = q.shape
    length = s // dil
    nb = length // BLK
    view = lambda a: a.reshape(b, length, dil * D_A)
    cur = pl.BlockSpec((None, BLK, D_A), lambda i, p, n: (i, n, p))
    prev = pl.BlockSpec((None, BLK, D_A), lambda i, p, n: (i, jnp.maximum(n - 1, 0), p))
    o, lse = pl.pallas_call(
        _attn_prompt_kernel,
        grid=(b, dil, nb),
        in_specs=[cur, prev, cur, prev, cur,
                  pl.BlockSpec(bias.shape, lambda i, p, n: (0, 0, 0))],
        out_specs=[cur, cur],
        out_shape=[jax.ShapeDtypeStruct((b, length, dil * D_A), F32)] * 2,
        compiler_params=_params("parallel", "parallel", "arbitrary"),
        name=f"attn_prompt_d{dil}",
    )(view(q), view(k), view(k), view(v), view(v), bias)
    return o.reshape(b, s, D_A), lse.reshape(b, s, D_A)


def _attn_sample_kernel(q_ref, kn_ref, vn_ref, b0_ref, ones_ref,
                        k1_ref, k4_ref, k16_ref, v1_ref, v4_ref, v16_ref,
                        br1_ref, br4_ref, br16_ref, o_ref):
    q = q_ref[...]
    bs = q.shape[0]
    ones = ones_ref[...]
    vn = vn_ref[...]
    l0 = _split_dot(q * kn_ref[...], ones) + b0_ref[...]
    outs, lses = [], []
    for k_ref, v_ref, br_ref in ((k1_ref, v1_ref, br1_ref), (k4_ref, v4_ref, br4_ref),
                                 (k16_ref, v16_ref, br16_ref)):
        prod = (k_ref[...] * q[:, None, :]).reshape(bs * BLK, D_A)
        lg = _split_dot(prod, ones).reshape(bs, BLK, D_A) + br_ref[...][None]
        m = jnp.maximum(jnp.max(lg, axis=1), l0)
        p = jnp.exp(lg - m[:, None, :])
        p0 = jnp.exp(l0 - m)
        s = jnp.sum(p, axis=1) + p0
        outs.append((jnp.sum(p * v_ref[...], axis=1) + p0 * vn) / s)
        lses.append(m + jnp.log(s))
    mm = jnp.maximum(jnp.maximum(lses[0], lses[1]), lses[2])
    ws = [jnp.exp(l - mm) for l in lses]
    tot = ws[0] + ws[1] + ws[2]
    o_ref[...] = (ws[0] * outs[0] + ws[1] * outs[1] + ws[2] * outs[2]) / tot


def _sample_bias_rows(rel_bias, dil):
    j = BLK - jnp.arange(BLK)
    return jnp.repeat(rel_bias[_t5_bucket(j * dil)].astype(F32), HEAD_DIM, axis=1)


def _attn_sample(q, kn, vn, cache_k, cache_v, layer, rel_bias):
    nseq = q.shape[0]
    depth, _, w_cache = cache_k.shape[:3]
    bs = BS_ATTN_SAMPLE
    specs, views_k, views_v, brs = [], [], [], []
    for window, dil in DILATED_PAIRS:
        assert window // dil == BLK and w_cache % (BLK * dil) == 0 and w_cache >= window
        nblk = w_cache // (BLK * dil)
        specs.append(pl.BlockSpec((None, bs, BLK, D_A), _cache_index(layer, nblk - 1)))
        views_k.append(cache_k.reshape(depth, nseq, w_cache // dil, dil * D_A))
        views_v.append(cache_v.reshape(depth, nseq, w_cache // dil, dil * D_A))
        brs.append(_sample_bias_rows(rel_bias, dil))
    b0 = jnp.repeat(rel_bias[0].astype(F32), HEAD_DIM)[None, :]
    row = pl.BlockSpec((bs, D_A), lambda i: (i, 0))
    full = lambda a: pl.BlockSpec(a.shape, lambda i: (0,) * a.ndim)
    ones = _head_ones()
    return pl.pallas_call(
        _attn_sample_kernel,
        grid=(nseq // bs,),
        in_specs=[row, row, row, full(b0), full(ones)] + specs + specs + [full(a) for a in brs],
        out_specs=row,
        out_shape=jax.ShapeDtypeStruct((nseq, D_A), F32),
        compiler_params=_params("parallel"),
        name="attn_sample",
    )(q, kn, vn, b0, ones, *views_k, *views_v, *brs)


def _cache_index(layer, blk):
    return lambda i: (layer, i, blk, 0)


def _lru_gates(xc, wa_ref, ba_ref, wx_ref, bx_ref, lam_ref):
    r = jax.nn.sigmoid(jnp.dot(xc, wa_ref[...], precision=HIGHEST, preferred_element_type=F32) + ba_ref[...])
    ig = jax.nn.sigmoid(jnp.dot(xc, wx_ref[...], precision=HIGHEST, preferred_element_type=F32) + bx_ref[...])
    nl = -lam_ref[...]
    softplus = jnp.maximum(nl, 0.0) + jnp.log(1.0 + jnp.exp(-jnp.abs(nl)))
    log_a = -LRU_C * r * softplus
    a = jnp.exp(log_a)
    bterm = jnp.sqrt(1.0 - a * a) * ig * xc
    return a, bterm


def _lru_prompt_kernel(xr_ref, xg_ref, cb_ref, h0_ref, cw_ref, cbias_ref, wa_ref, ba_ref, wx_ref, bx_ref,
                       lam_ref, y_ref, nb_ref, hl_ref, xbuf, a_s, b_s, h_s, hcar):
    tt = xr_ref.shape[0]
    pad = 8

    @pl.when(pl.program_id(1) == 0)
    def _():
        xbuf[0:pad, :] = jnp.zeros((pad, D_C), F32)
        xbuf[pad - (CONV_W - 1):pad, :] = cb_ref[...]
        hcar[...] = h0_ref[...]

    xbuf[pad:pad + tt, :] = xr_ref[...]
    xc = cbias_ref[...]
    for kk in range(CONV_W):
        off = pad - (CONV_W - 1) + kk
        xc = xc + xbuf[off:off + tt, :] * cw_ref[kk:kk + 1, :]
    a, bterm = _lru_gates(xc, wa_ref, ba_ref, wx_ref, bx_ref, lam_ref)
    a_s[...] = a
    b_s[...] = bterm

    def step(i, h):
        h = a_s[pl.ds(i, 1), :] * h + b_s[pl.ds(i, 1), :]
        h_s[pl.ds(i, 1), :] = h
        return h

    h = lax.fori_loop(0, tt, step, hcar[...], unroll=8)
    hcar[...] = h
    hl_ref[...] = h
    y_ref[...] = jax.nn.gelu(xg_ref[...]) * h_s[...]
    tail = xbuf[pad + tt - (CONV_W - 1):pad + tt, :]
    nb_ref[...] = tail
    xbuf[pad - (CONV_W - 1):pad, :] = tail


def _lru_weights(lp):
    def blockdiag(w):
        eye = jnp.eye(H_C, dtype=w.dtype)
        return jnp.einsum('hij,hg->higj', w, eye).reshape(D_C, D_C)
    return [lp['conv_w'], lp['conv_b'][None, :], blockdiag(lp['w_a']), lp['b_a'][None, :],
            blockdiag(lp['w_x']), lp['b_x'][None, :], lp['lru_lambda'][None, :]]


def _lru_prompt(xr, xg, conv_buf, h0, lp):
    b, t, _ = xr.shape
    tt = min(TT_LRU, t)
    assert t % tt == 0 and tt >= CONV_W - 1
    weights = _lru_weights(lp)
    row = pl.BlockSpec((None, tt, D_C), lambda i, j: (i, j, 0))
    per_seq = lambda n: pl.BlockSpec((None, n, D_C), lambda i, j: (i, 0, 0))
    full = lambda a: pl.BlockSpec(a.shape, lambda i, j: (0,) * a.ndim)
    y, nb, hl = pl.pallas_call(
        _lru_prompt_kernel,
        grid=(b, t // tt),
        in_specs=[row, row, per_seq(CONV_W - 1), per_seq(1)] + [full(w) for w in weights],
        out_specs=[row, per_seq(CONV_W - 1), per_seq(1)],
        out_shape=[jax.ShapeDtypeStruct((b, t, D_C), F32),
                   jax.ShapeDtypeStruct((b, CONV_W - 1, D_C), F32),
                   jax.ShapeDtypeStruct((b, 1, D_C), F32)],
        scratch_shapes=[pltpu.VMEM((tt + 8, D_C), F32), pltpu.VMEM((tt, D_C), F32),
                        pltpu.VMEM((tt, D_C), F32), pltpu.VMEM((tt, D_C), F32),
                        pltpu.VMEM((1, D_C), F32)],
        compiler_params=_params("parallel", "arbitrary"),
        name="lru_prompt",
    )(xr, xg, conv_buf, h0[:, None, :], *weights)
    return y, nb, hl[:, 0, :]


def _lru_sample_kernel(xr_ref, xg_ref, c0_ref, c1_ref, c2_ref, h0_ref, cw_ref, cbias_ref, wa_ref, ba_ref,
                       wx_ref, bx_ref, lam_ref, y_ref, hl_ref):
    xc = (cbias_ref[...] + c0_ref[...] * cw_ref[0:1, :] + c1_ref[...] * cw_ref[1:2, :]
          + c2_ref[...] * cw_ref[2:3, :] + xr_ref[...] * cw_ref[3:4, :])
    a, bterm = _lru_gates(xc, wa_ref, ba_ref, wx_ref, bx_ref, lam_ref)
    h = a * h0_ref[...] + bterm
    hl_ref[...] = h
    y_ref[...] = jax.nn.gelu(xg_ref[...]) * h


def _lru_sample(xr, xg, conv_buf, h0, lp):
    n = xr.shape[0]
    weights = _lru_weights(lp)
    args = [xr, xg, conv_buf[:, 0], conv_buf[:, 1], conv_buf[:, 2], h0] + weights
    full = lambda a: pl.BlockSpec(a.shape, lambda i: (0,) * a.ndim)
    return pl.pallas_call(
        _lru_sample_kernel,
        grid=(1,),
        in_specs=[full(a) for a in args],
        out_specs=[full(xr), full(xr)],
        out_shape=[jax.ShapeDtypeStruct((n, D_C), F32)] * 2,
        compiler_params=_params("arbitrary"),
        name="lru_sample",
    )(*args)


def _outproj_kernel(*refs, merge):
    if merge:
        o1, o2, o3, l1, l2, l3 = refs[:6]
        refs = refs[6:]
        lses = [l1[...], l2[...], l3[...]]
        mm = jnp.maximum(jnp.maximum(lses[0], lses[1]), lses[2])
        ws = [jnp.exp(l - mm) for l in lses]
        ya = (ws[0] * o1[...] + ws[1] * o2[...] + ws[2] * o3[...]) / (ws[0] + ws[1] + ws[2])
    else:
        ya = refs[0][...]
        refs = refs[1:]
    yb_ref, yc_ref, x_ref, g1_ref, sh_ref, sc_ref, og_ref, w_ref, nf_ref, xm_ref, h2_ref = refs
    og = og_ref[...]
    mix = jnp.concatenate([_rms_rows(ya) * og[:, 0:D_A],
                           _rms_rows(yb_ref[...]) * og[:, D_A:D_A + D_B],
                           _rms_rows(yc_ref[...]) * og[:, D_A + D_B:]], axis=-1)
    x = x_ref[...] + g1_ref[...] * jnp.dot(mix.astype(BF16), w_ref[...], preferred_element_type=F32)
    xm_ref[...] = x
    h2 = _rms_rows(x) * nf_ref[...] * (1.0 + sc_ref[...]) + sh_ref[...]
    h2_ref[...] = h2.astype(BF16)


def _outproj(attn, yb, yc, x, g1, sh2, sc2, lp):
    g, t, _ = x.shape
    tm = min(TM_PROJ, t)
    merge = len(attn) > 1
    r = g1.shape[1]
    row = lambda d: pl.BlockSpec((None, tm, d), lambda b, i: (b, i, 0))
    mod_spec = pl.BlockSpec((None, r, D_MODEL), lambda b, i: (b, 0, 0))
    full = lambda a: pl.BlockSpec(a.shape, lambda b, i: (0,) * a.ndim)
    weights = [lp['out_gain'][None, :], lp['w_out'].astype(BF16), lp['norm_ffn'][None, :]]
    return pl.pallas_call(
        functools.partial(_outproj_kernel, merge=merge),
        grid=(g, t // tm),
        in_specs=[row(D_A)] * len(attn) + [row(D_B), row(D_C), row(D_MODEL), mod_spec, mod_spec, mod_spec]
                 + [full(w) for w in weights],
        out_specs=[row(D_MODEL), row(D_MODEL)],
        out_shape=[jax.ShapeDtypeStruct((g, t, D_MODEL), F32), jax.ShapeDtypeStruct((g, t, D_MODEL), BF16)],
        compiler_params=_params("parallel", "parallel"),
        name="outproj_prompt" if merge else "outproj_sample",
    )(*attn, yb, yc, x, g1, sh2, sc2, *weights)


def _sort16_pairs():
    pairs = []

    def merge(lo, hi, r):
        step = r * 2
        if step < hi - lo:
            merge(lo, hi, step)
            merge(lo + r, hi, step)
            pairs.extend((i, i + r) for i in range(lo + r, hi - r, step))
        else:
            pairs.append((lo, lo + r))

    def sort(lo, hi):
        if hi - lo >= 1:
            mid = lo + (hi - lo) // 2
            sort(lo, mid)
            sort(mid + 1, hi)
            merge(lo, hi, 1)

    sort(0, PEER_TOPK - 1)
    return pairs


_SORT16 = _sort16_pairs()
_BITONIC16 = [(i, i + d) for d in (8, 4, 2, 1) for i in range(PEER_TOPK) if not i & d]


def _exchange(vals, pairs):
    vals = list(vals)
    for i, j in pairs:
        hi, lo = jnp.maximum(vals[i], vals[j]), jnp.minimum(vals[i], vals[j])
        vals[i], vals[j] = hi, lo
    return vals


def _merge_top16(a, b):
    return _exchange([jnp.maximum(a[i], b[PEER_TOPK - 1 - i]) for i in range(PEER_TOPK)], _BITONIC16)


def _across_sublanes(vals):
    for shift in (4, 2, 1):
        vals = _merge_top16(vals, [pltpu.roll(v, shift, axis=0) for v in vals])
    return vals


def _top16_of_keys(s):
    slabs = [s[8 * i:8 * i + 8, :] for i in range(N_KEYS // 8)]
    return _across_sublanes(_exchange(slabs, _SORT16))


def _peer_select_kernel(xt_ref, wq_ref, keys_ref, th_ref, e1_ref, s2_ref, e2_ref):
    xt = xt_ref[...]
    tn = xt.shape[1]
    sub = lax.broadcasted_iota(jnp.int32, (8, tn), 0)

    def head(h, carry):
        s = []
        for p in range(2):
            qt = jnp.dot(wq_ref[2 * h + p], xt, preferred_element_type=F32)
            qn = qt * lax.rsqrt(jnp.mean(qt * qt, axis=0, keepdims=True) + EPS)
            s.append(jnp.dot(keys_ref[2 * h + p], qn, precision=HIGHEST, preferred_element_type=F32))
        s1, s2 = s
        v1 = _top16_of_keys(s1)
        v2 = _top16_of_keys(s2)
        a_lo, a_hi = v1[0], v1[8]
        for j in range(1, 8):
            a_lo = jnp.where(sub == j, v1[j], a_lo)
            a_hi = jnp.where(sub == j, v1[8 + j], a_hi)
        sv = _across_sublanes(_merge_top16([a_lo + b for b in v2], [a_hi + b for b in v2]))
        tau = sv[PEER_TOPK - 1]
        z = jnp.ones_like(tau)
        for kk in range(1, PEER_TOPK):
            z = z + jnp.exp(sv[kk] - sv[0])
        inv_z = 1.0 / z
        for i in range(N_KEYS // 8):
            rows = slice(8 * i, 8 * i + 8)
            s1r = s1[rows]
            th = jnp.full((8, tn), BIG, F32)
            for b in range(PEER_TOPK):
                th = jnp.where(s1r + v2[b] >= tau, v2[b], th)
            th_ref[h, rows, :] = th
            e1_ref[h, rows, :] = jnp.exp(s1r - v1[0])
            s2_ref[h, rows, :] = s2[rows]
            e2_ref[h, rows, :] = jnp.exp(s2[rows] - v2[0]) * inv_z
        return carry

    lax.fori_loop(0, PEER_HEADS, head, 0)


def _peer_select(h2t, lp, tn):
    n = h2t.shape[1]
    wq = lp['peer_wq'].T.reshape(2 * PEER_HEADS, D_QUERY // 2, D_MODEL).astype(BF16)
    keys = lp['peer_keys'].reshape(2 * PEER_HEADS, N_KEYS, D_QUERY // 2)
    out = pl.BlockSpec((PEER_HEADS, N_KEYS, tn), lambda i: (0, 0, i))
    return pl.pallas_call(
        _peer_select_kernel,
        grid=(n // tn,),
        in_specs=[pl.BlockSpec((D_MODEL, tn), lambda i: (0, i)),
                  pl.BlockSpec(wq.shape, lambda i: (0, 0, 0)),
                  pl.BlockSpec(keys.shape, lambda i: (0, 0, 0))],
        out_specs=[out] * 4,
        out_shape=[jax.ShapeDtypeStruct((PEER_HEADS, N_KEYS, n), F32)] * 4,
        compiler_params=_params("parallel"),
        name="peer_select",
    )(h2t, wq, keys)


def _peer_dense_kernel(xt_ref, u_ref, vt_ref, th_ref, e1_ref, s2_ref, e2_ref, xm_ref, g2_ref, o_ref,
                       acc, m_s):
    j = pl.program_id(1)

    @pl.when(j == 0)
    def _():
        acc[...] = jnp.zeros_like(acc)

    at = jnp.dot(u_ref[...], xt_ref[...], preferred_element_type=F32)
    for rr in range(u_ref.shape[0] // N_KEYS):
        r = j * (u_ref.shape[0] // N_KEYS) + rr
        rows = slice(rr * N_KEYS, (rr + 1) * N_KEYS)
        gate = jnp.zeros((N_KEYS, at.shape[1]), F32)
        for h in range(PEER_HEADS):
            th = th_ref[h, pl.ds(r, 1), :]
            e1 = e1_ref[h, pl.ds(r, 1), :]
            gate = gate + jnp.where(s2_ref[h] >= th, e2_ref[h], 0.0) * e1
        m_s[rows, :] = (gate * jax.nn.gelu(at[rows])).astype(BF16)
    acc[...] += jnp.dot(vt_ref[...], m_s[...], preferred_element_type=F32)

    @pl.when(j == pl.num_programs(1) - 1)
    def _():
        o_ref[...] = xm_ref[...] + g2_ref[...] * acc[...].T


def _peer_dense(h2t, sel, xmid, g2, lp, tn):
    g, t, _ = xmid.shape
    n = g * t
    te = TE_DENSE
    n_exp = lp['peer_u'].shape[0]
    u = lp['peer_u'].astype(BF16)
    vt = lp['peer_v'].T.astype(BF16)
    tiles_per_seq = t // tn
    r = g2.shape[1]
    tok = pl.BlockSpec((PEER_HEADS, N_KEYS, tn), lambda i, j: (0, 0, i))
    row = pl.BlockSpec((None, tn, D_MODEL), lambda i, j: (i // tiles_per_seq, i % tiles_per_seq, 0))
    return pl.pallas_call(
        _peer_dense_kernel,
        grid=(n // tn, n_exp // te),
        in_specs=[pl.BlockSpec((D_MODEL, tn), lambda i, j: (0, i)),
                  pl.BlockSpec((te, D_MODEL), lambda i, j: (j, 0)),
                  pl.BlockSpec((D_MODEL, te), lambda i, j: (0, j)),
                  tok, tok, tok, tok, row,
                  pl.BlockSpec((None, r, D_MODEL), lambda i, j: (i // tiles_per_seq, 0, 0))],
        out_specs=row,
        out_shape=jax.ShapeDtypeStruct((g, t, D_MODEL), F32),
        scratch_shapes=[pltpu.VMEM((D_MODEL, tn), F32), pltpu.VMEM((te, tn), BF16)],
        compiler_params=_params("parallel", "arbitrary"),
        name="peer_dense",
    )(h2t, u, vt, *sel, xmid, g2)


def _peer(xmid, h2, g2, lp, tn_select, tn_dense):
    g, t, _ = xmid.shape
    h2t = h2.reshape(g * t, D_MODEL).T
    sel = _peer_select(h2t, lp, tn_select)
    return _peer_dense(h2t, sel, xmid, g2, lp, tn_dense)


def _layer_prompt(x, mod, lp, biases):
    sh1, sc1, g1, sh2, sc2, g2 = mod
    b, t, _ = x.shape
    q, k, v, yb, gv, xr, xg = _inproj(x, sh1, sc1, lp, chunked=True)
    branches = [_attn_prompt(q, k, v, bias, dil) for bias, (_, dil) in zip(biases, DILATED_PAIRS)]
    attn = [o for o, _ in branches] + [l for _, l in branches]
    yc, new_buf, h_last = _lru_prompt(xr, xg, jnp.zeros((b, CONV_W - 1, D_C), F32), jnp.zeros((b, D_C), F32), lp)
    xmid, h2 = _outproj(attn, yb, yc, x, g1, sh2, sc2, lp)
    x = _peer(xmid, h2, g2, lp, TN_SELECT, TN_DENSE)
    keep = min(W_MAX, t)
    state = (k[:, t - keep:].reshape(b, keep, H_A, HEAD_DIM), v[:, t - keep:].reshape(b, keep, H_A, HEAD_DIM),
             gv[:, t - CHUNK:], new_buf, h_last)
    return x, state


def _layer_sample(x, mod, lp, rel_bias, cache_k, cache_v, conv_buf, h0, layer):
    sh1, sc1, g1, sh2, sc2, g2 = mod
    _, n, _ = x.shape
    q, k, v, yb, gv, xr, xg = _inproj(x, sh1, sc1, lp, chunked=False)
    ya = _attn_sample(q[0], k[0], v[0], cache_k, cache_v, layer, rel_bias)
    yc, h_last = _lru_sample(xr[0], xg[0], conv_buf, h0, lp)
    xmid, h2 = _outproj([ya[None]], yb, yc[None], x, g1, sh2, sc2, lp)
    x = _peer(xmid, h2, g2, lp, n, n)
    new_buf = jnp.concatenate([conv_buf[:, 1:], xr[0][:, None, :]], axis=1)
    state = (k[0].reshape(n, 1, H_A, HEAD_DIM), v[0].reshape(n, 1, H_A, HEAD_DIM), gv[0][:, None, :],
             new_buf, h_last)
    return x, state


def kernel(x_prompt, x_sample, cache_k, cache_v, state_conv, state_h, c_prompt, c_sample, rel_bias, w_ada, b_ada, norm_mix, norm_ffn, w_in, q_gain, k_gain, gmlp_norm, w_s, b_s, conv_w, conv_b, w_a, b_a, w_x, b_x, lru_lambda, out_gain, w_out, peer_wq, peer_keys, peer_u, peer_v):
    depth = w_in.shape[0]
    nb, ns = x_prompt.shape[0], x_sample.shape[0]
    assert x_sample.shape[1] == 1
    pad = -(nb + ns) % 8
    c_all = jnp.concatenate([c_prompt, c_sample, jnp.zeros((pad, D_MODEL), F32)], axis=0)
    mods = _ada(c_all, w_ada, b_ada)
    biases = [_prompt_bias(rel_bias, window, dil) for window, dil in DILATED_PAIRS]
    xp = x_prompt
    xs = x_sample.reshape(1, ns, D_MODEL)
    st_p, st_s = [], []
    for l in range(depth):
        lp = dict(norm_mix=norm_mix[l], norm_ffn=norm_ffn[l], w_in=w_in[l], q_gain=q_gain[l], k_gain=k_gain[l],
                  gmlp_norm=gmlp_norm[l], w_s=w_s[l], b_s=b_s[l], conv_w=conv_w[l], conv_b=conv_b[l],
                  w_a=w_a[l], b_a=b_a[l], w_x=w_x[l], b_x=b_x[l], lru_lambda=lru_lambda[l],
                  out_gain=out_gain[l], w_out=w_out[l], peer_wq=peer_wq[l], peer_keys=peer_keys[l],
                  peer_u=peer_u[l], peer_v=peer_v[l])
        mod_p = [m[:, None, :] for m in jnp.split(mods[l, :nb], 6, axis=-1)]
        mod_s = [m[None] for m in jnp.split(mods[l, nb:nb + ns], 6, axis=-1)]
        xp, sp = _layer_prompt(xp, mod_p, lp, biases)
        xs, ss = _layer_sample(xs, mod_s, lp, rel_bias, cache_k, cache_v, state_conv[l], state_h[l], l)
        st_p.append(sp)
        st_s.append(ss)
    stack = lambda sts, i: jnp.stack([s[i] for s in sts])
    return (xp, xs.reshape(ns, 1, D_MODEL),
            stack(st_p, 0), stack(st_p, 1), stack(st_s, 0), stack(st_s, 1),
            stack(st_p, 2), stack(st_s, 2), stack(st_p, 3), stack(st_s, 3),
            stack(st_p, 4), stack(st_s, 4))
```

```python
import functools
import math

import jax
import jax.numpy as jnp
import numpy as np
from jax import lax
from jax.experimental import pallas as pl
from jax.experimental.pallas import tpu as pltpu

F32 = jnp.float32
BF16 = jnp.bfloat16
HIGHEST = lax.Precision.HIGHEST

D_MODEL = 1024
HEAD_DIM = 64
D_A = 512
H_A = 8
DILATED_PAIRS = ((128, 1), (512, 4), (2048, 16))
W_MAX = 2048
BLK = 128
ATTN_SCALE = HEAD_DIM ** -0.5
N_BUCKETS = 32
MAX_DIST = 2048
D_B = 256
G_B = 4
E_B = 64
CHUNK = 128
D_C = 256
H_C = 4
E_C = 64
CONV_W = 4
LRU_C = 8.0
PEER_HEADS = 8
N_KEYS = 128
PEER_TOPK = 16
D_QUERY = 256
D_IN = 3 * D_A + 2 * D_B + 2 * D_C
EPS = 1e-6
NEG = -1e30
BIG = 3.0e38

VMEM_LIMIT_BYTES = 56 * 1024 * 1024

TM_PROJ = 512
TT_LRU = 512
TN_SELECT = 256
TN_DENSE = 512
TE_DENSE = 512
BS_ATTN_SAMPLE = 4


def _params(*sem):
    return pltpu.CompilerParams(dimension_semantics=sem, vmem_limit_bytes=VMEM_LIMIT_BYTES)


def _split_dot(x, w_bf16):
    hi = x.astype(BF16)
    lo = (x - hi.astype(F32)).astype(BF16)
    return (jnp.dot(hi, w_bf16, preferred_element_type=F32)
            + jnp.dot(lo, w_bf16, preferred_element_type=F32))


def _rms_rows(x):
    return x * lax.rsqrt(jnp.mean(x * x, axis=-1, keepdims=True) + EPS)


def _ada_kernel(c_ref, w_ref, b_ref, o_ref):
    c = c_ref[...]
    o_ref[...] = jnp.dot(jax.nn.silu(c), w_ref[...], precision=HIGHEST,
                         preferred_element_type=F32) + b_ref[...]


def _ada(c_all, w_ada, b_ada):
    depth, _, n_out = w_ada.shape
    rows = c_all.shape[0]
    tn = 512
    return pl.pallas_call(
        _ada_kernel,
        grid=(depth, n_out // tn),
        in_specs=[pl.BlockSpec((rows, D_MODEL), lambda l, j: (0, 0)),
                  pl.BlockSpec((None, D_MODEL, tn), lambda l, j: (l, 0, j)),
                  pl.BlockSpec((None, 1, tn), lambda l, j: (l, 0, j))],
        out_specs=pl.BlockSpec((None, rows, tn), lambda l, j: (l, 0, j)),
        out_shape=jax.ShapeDtypeStruct((depth, rows, n_out), F32),
        compiler_params=_params("parallel", "parallel"),
        name="ada",
    )(c_all, w_ada, b_ada.reshape(depth, 1, n_out))


def _inproj_kernel(x_ref, sh_ref, sc_ref, nm_ref, w_ref, qg_ref, kg_ref, gn_ref, ones_ref,
                   gw_ref, gb_ref, q_ref, k_ref, v_ref, yb_ref, gv_ref, xr_ref, xg_ref, *, chunked):
    x = x_ref[...]
    h = _rms_rows(x) * nm_ref[...] * (1.0 + sc_ref[...]) + sh_ref[...]
    z = jnp.dot(h.astype(BF16), w_ref[...], preferred_element_type=F32)
    ones = ones_ref[...]

    def headnorm(a, g):
        ss = _split_dot(a * a, ones)
        return a * lax.rsqrt(ss * (1.0 / HEAD_DIM) + EPS) * g

    q_ref[...] = headnorm(z[:, 0:D_A], qg_ref[...]) * ATTN_SCALE
    k_ref[...] = headnorm(z[:, D_A:2 * D_A], kg_ref[...])
    v_ref[...] = z[:, 2 * D_A:3 * D_A]
    o = 3 * D_A
    ub = z[:, o:o + D_B]
    vbn = _rms_rows(z[:, o + D_B:o + 2 * D_B]) * gn_ref[...]
    gv_ref[...] = vbn
    xr_ref[...] = z[:, o + 2 * D_B:o + 2 * D_B + D_C]
    xg_ref[...] = z[:, o + 2 * D_B + D_C:o + 2 * D_B + 2 * D_C]
    if chunked:
        rg = lax.broadcasted_iota(jnp.int32, (G_B * CHUNK, D_B), 0) // CHUNK
        cg = lax.broadcasted_iota(jnp.int32, (G_B * CHUNK, D_B), 1) // E_B
        keep = rg == cg
        for ci in range(x.shape[0] // CHUNK):
            rows = slice(ci * CHUNK, (ci + 1) * CHUNK)
            vc = vbn[rows]
            vbd = jnp.where(keep, jnp.concatenate([vc] * G_B, axis=0), 0.0)
            mix = _split_dot_rhs(gw_ref[...], vbd) + gb_ref[...]
            yb_ref[rows, :] = ub[rows] * mix
    else:
        yb_ref[...] = ub * (gw_ref[...] * vbn + gb_ref[...])


def _split_dot_rhs(w_bf16_pair, x):
    w_hi, w_lo = w_bf16_pair[0], w_bf16_pair[1]
    x_hi = x.astype(BF16)
    x_lo = (x - x_hi.astype(F32)).astype(BF16)
    return (jnp.dot(w_hi, x_hi, preferred_element_type=F32)
            + jnp.dot(w_hi, x_lo, preferred_element_type=F32)
            + jnp.dot(w_lo, x_hi, preferred_element_type=F32))


def _bf16_pair(w):
    hi = w.astype(BF16)
    lo = (w - hi.astype(F32)).astype(BF16)
    return jnp.stack([hi, lo])


def _inproj(x, sh, sc, lp, *, chunked):
    g, t, _ = x.shape
    tm = min(TM_PROJ, t)
    r = sh.shape[1]
    mod_spec = pl.BlockSpec((None, r, D_MODEL), (lambda b, i: (b, 0, 0)))
    full = lambda a: pl.BlockSpec(a.shape, lambda b, i: (0,) * a.ndim)
    if chunked:
        gw = _bf16_pair(jnp.transpose(jnp.tril(lp['w_s']), (1, 0, 2)).reshape(CHUNK, G_B * CHUNK))
        gb = jnp.repeat(lp['b_s'].T, E_B, axis=1)
    else:
        gw = jnp.repeat(lp['w_s'][:, 0, 0], E_B)[None, :]
        gb = jnp.repeat(lp['b_s'][:, 0], E_B)[None, :]
    weights = [lp['norm_mix'][None, :], lp['w_in'].astype(BF16),
               jnp.tile(lp['q_gain'], H_A)[None, :], jnp.tile(lp['k_gain'], H_A)[None, :],
               lp['gmlp_norm'][None, :], _head_ones(), gw, gb]
    row = lambda d: pl.BlockSpec((None, tm, d), lambda b, i: (b, i, 0))
    widths = (D_A, D_A, D_A, D_B, D_B, D_C, D_C)
    return pl.pallas_call(
        functools.partial(_inproj_kernel, chunked=chunked),
        grid=(g, t // tm),
        in_specs=[row(D_MODEL), mod_spec, mod_spec] + [full(w) for w in weights],
        out_specs=[row(d) for d in widths],
        out_shape=[jax.ShapeDtypeStruct((g, t, d), F32) for d in widths],
        compiler_params=_params("parallel", "parallel"),
        name="inproj_prompt" if chunked else "inproj_sample",
    )(x, sh, sc, *weights)


def _head_ones():
    hid = np.arange(D_A) // HEAD_DIM
    return jnp.asarray((hid[:, None] == hid[None, :]).astype(np.float32), dtype=BF16)


def _t5_bucket(dist):
    exact = N_BUCKETS // 2
    df = np.maximum(dist, 1).astype(np.float64)
    large = exact + (np.log(df / exact) / math.log(MAX_DIST / exact) * (N_BUCKETS - exact)).astype(np.int64)
    return np.where(dist < exact, dist, np.minimum(large, N_BUCKETS - 1))


def _bias_lookup(rel_bias, dist):
    onehot = (_t5_bucket(dist)[..., None] == np.arange(N_BUCKETS)).astype(np.float32)
    return jnp.einsum('...b,bh->...h', jnp.asarray(onehot), rel_bias.astype(F32), precision=HIGHEST)


def _prompt_bias(rel_bias, window, dil):
    qi = np.arange(BLK)[:, None] + BLK
    ki = np.arange(2 * BLK)[None, :]
    dm = qi - ki
    valid = (dm >= 0) & (dm <= window // dil)
    bias = jnp.transpose(_bias_lookup(rel_bias, np.maximum(dm, 0) * dil), (2, 0, 1))
    return jnp.where(jnp.asarray(valid)[None], bias, NEG)


def _attn_prompt_kernel(q_ref, kp_ref, kc_ref, vp_ref, vc_ref, bias_ref, o_ref, lse_ref):
    first = pl.program_id(2) == 0
    q = q_ref[...].astype(BF16)
    k2 = jnp.concatenate([kp_ref[...], kc_ref[...]], axis=0).astype(BF16)
    v2 = jnp.concatenate([vp_ref[...], vc_ref[...]], axis=0).astype(BF16)
    col = lax.broadcasted_iota(jnp.int32, (BLK, 2 * BLK), 1)
    no_prev = jnp.logical_and(first, col < BLK)
    for h in range(H_A):
        hs = slice(h * HEAD_DIM, (h + 1) * HEAD_DIM)
        lg = lax.dot_general(q[:, hs], k2[:, hs], (((1,), (1,)), ((), ())),
                             preferred_element_type=F32) + bias_ref[h]
        lg = jnp.where(no_prev, NEG, lg)
        m = jnp.max(lg, axis=-1, keepdims=True)
        p = jnp.exp(lg - m)
        s = jnp.sum(p, axis=-1, keepdims=True)
        o = jnp.dot(p.astype(BF16), v2[:, hs], preferred_element_type=F32) / s
        o_ref[:, hs] = o
        lse_ref[:, hs] = jnp.broadcast_to(m + jnp.log(s), (BLK, HEAD_DIM))


def _attn_prompt(q, k, v, bias, dil):
    b, s, _ = q.shape
    length = s // dil
    nb = length // BLK
    view = lambda a: a.reshape(b, length, dil * D_A)
    cur = pl.BlockSpec((None, BLK, D_A), lambda i, p, n: (i, n, p))
    prev = pl.BlockSpec((None, BLK, D_A), lambda i, p, n: (i, jnp.maximum(n - 1, 0), p))
    o, lse = pl.pallas_call(
        _attn_prompt_kernel,
        grid=(b, dil, nb),
        in_specs=[cur, prev, cur, prev, cur,
                  pl.BlockSpec(bias.shape, lambda i, p, n: (0, 0, 0))],
        out_specs=[cur, cur],
        out_shape=[jax.ShapeDtypeStruct((b, length, dil * D_A), F32)] * 2,
        compiler_params=_params("parallel", "parallel", "arbitrary"),
        name=f"attn_prompt_d{dil}",
    )(view(q), view(k), view(k), view(v), view(v), bias)
    return o.reshape(b, s, D_A), lse.reshape(b, s, D_A)


def _attn_sample_kernel(q_ref, kn_ref, vn_ref, b0_ref, ones_ref,
                        k1_ref, k4_ref, k16_ref, v1_ref, v4_ref, v16_ref,
                        br1_ref, br4_ref, br16_ref, o_ref):
    q = q_ref[...]
    ones = ones_ref[...]
    vn = vn_ref[...]

    def headsum(x):
        return _split_dot(x.reshape(-1, HEAD_DIM), ones).reshape(x.shape)

    l0 = headsum(q * kn_ref[...]) + b0_ref[...]
    outs, lses = [], []
    for k_ref, v_ref, br_ref in ((k1_ref, v1_ref, br1_ref), (k4_ref, v4_ref, br4_ref),
                                 (k16_ref, v16_ref, br16_ref)):
        lg = headsum(k_ref[...] * q[:, None]) + br_ref[...][None]
        m = jnp.maximum(jnp.max(lg, axis=1), l0)
        p = jnp.exp(lg - m[:, None])
        p0 = jnp.exp(l0 - m)
        s = jnp.sum(p, axis=1) + p0
        outs.append((jnp.sum(p * v_ref[...], axis=1) + p0 * vn) / s)
        lses.append(m + jnp.log(s))
    mm = jnp.maximum(jnp.maximum(lses[0], lses[1]), lses[2])
    ws = [jnp.exp(l - mm) for l in lses]
    tot = ws[0] + ws[1] + ws[2]
    o_ref[...] = (ws[0] * outs[0] + ws[1] * outs[1] + ws[2] * outs[2]) / tot


def _attn_sample(q, kn, vn, cache_k, cache_v, layer, rel_bias):
    nseq = q.shape[0]
    depth, _, w_cache = cache_k.shape[:3]
    bs = BS_ATTN_SAMPLE
    heads = lambda a: a.reshape(nseq, H_A, HEAD_DIM)
    specs, views_k, views_v, brs = [], [], [], []
    for window, dil in DILATED_PAIRS:
        assert window // dil == BLK and w_cache % (BLK * dil) == 0 and w_cache >= window
        last = w_cache // (BLK * dil) - 1
        specs.append(pl.BlockSpec((None, bs, BLK, None, H_A, HEAD_DIM),
                                  functools.partial(_cache_index, layer=layer, blk=last)))
        views_k.append(cache_k.reshape(depth, nseq, w_cache // dil, dil, H_A, HEAD_DIM))
        views_v.append(cache_v.reshape(depth, nseq, w_cache // dil, dil, H_A, HEAD_DIM))
        rows = _bias_lookup(rel_bias, (BLK - np.arange(BLK)) * dil)
        brs.append(jnp.broadcast_to(rows[:, :, None], (BLK, H_A, HEAD_DIM)))
    b0 = jnp.broadcast_to(rel_bias[0].astype(F32)[:, None], (H_A, HEAD_DIM))
    row = pl.BlockSpec((bs, H_A, HEAD_DIM), lambda i: (i, 0, 0))
    full = lambda a: pl.BlockSpec(a.shape, lambda i: (0,) * a.ndim)
    ones = jnp.ones((HEAD_DIM, HEAD_DIM), BF16)
    out = pl.pallas_call(
        _attn_sample_kernel,
        grid=(nseq // bs,),
        in_specs=[row, row, row, full(b0), full(ones)] + specs + specs + [full(a) for a in brs],
        out_specs=row,
        out_shape=jax.ShapeDtypeStruct((nseq, H_A, HEAD_DIM), F32),
        compiler_params=_params("parallel"),
        name="attn_sample",
    )(heads(q), heads(kn), heads(vn), b0, ones, *views_k, *views_v, *brs)
    return out.reshape(nseq, D_A)


def _cache_index(i, *, layer, blk):
    return (layer, i, blk, 0, 0, 0)


def _lru_gates(xc, wa_ref, ba_ref, wx_ref, bx_ref, lam_ref):
    r = jax.nn.sigmoid(jnp.dot(xc, wa_ref[...], precision=HIGHEST, preferred_element_type=F32) + ba_ref[...])
    ig = jax.nn.sigmoid(jnp.dot(xc, wx_ref[...], precision=HIGHEST, preferred_element_type=F32) + bx_ref[...])
    nl = -lam_ref[...]
    softplus = jnp.maximum(nl, 0.0) + jnp.log(1.0 + jnp.exp(-jnp.abs(nl)))
    log_a = -LRU_C * r * softplus
    a = jnp.exp(log_a)
    bterm = jnp.sqrt(1.0 - a * a) * ig * xc
    return a, bterm


def _lru_prompt_kernel(xr_ref, xg_ref, cb_ref, h0_ref, cw_ref, cbias_ref, wa_ref, ba_ref, wx_ref, bx_ref,
                       lam_ref, y_ref, nb_ref, hl_ref, xbuf, a_s, b_s, h_s, hcar):
    tt = xr_ref.shape[0]
    pad = 8

    @pl.when(pl.program_id(1) == 0)
    def _():
        xbuf[0:pad, :] = jnp.zeros((pad, D_C), F32)
        xbuf[pad - (CONV_W - 1):pad, :] = cb_ref[...]
        hcar[...] = h0_ref[...]

    xbuf[pad:pad + tt, :] = xr_ref[...]
    xc = cbias_ref[...]
    for kk in range(CONV_W):
        off = pad - (CONV_W - 1) + kk
        xc = xc + xbuf[off:off + tt, :] * cw_ref[kk:kk + 1, :]
    a, bterm = _lru_gates(xc, wa_ref, ba_ref, wx_ref, bx_ref, lam_ref)
    a_s[...] = a
    b_s[...] = bterm

    def step(i, h):
        h = a_s[pl.ds(i, 1), :] * h + b_s[pl.ds(i, 1), :]
        h_s[pl.ds(i, 1), :] = h
        return h

    h = lax.fori_loop(0, tt, step, hcar[...], unroll=8)
    hcar[...] = h
    hl_ref[...] = h
    y_ref[...] = jax.nn.gelu(xg_ref[...]) * h_s[...]
    tail = xbuf[pad + tt - (CONV_W - 1):pad + tt, :]
    nb_ref[...] = tail
    xbuf[pad - (CONV_W - 1):pad, :] = tail


def _lru_weights(lp):
    def blockdiag(w):
        eye = jnp.eye(H_C, dtype=w.dtype)
        return jnp.einsum('hij,hg->higj', w, eye).reshape(D_C, D_C)
    return [lp['conv_w'], lp['conv_b'][None, :], blockdiag(lp['w_a']), lp['b_a'][None, :],
            blockdiag(lp['w_x']), lp['b_x'][None, :], lp['lru_lambda'][None, :]]


def _lru_prompt(xr, xg, conv_buf, h0, lp):
    b, t, _ = xr.shape
    tt = min(TT_LRU, t)
    assert t % tt == 0 and tt >= CONV_W - 1
    weights = _lru_weights(lp)
    row = pl.BlockSpec((None, tt, D_C), lambda i, j: (i, j, 0))
    per_seq = lambda n: pl.BlockSpec((None, n, D_C), lambda i, j: (i, 0, 0))
    full = lambda a: pl.BlockSpec(a.shape, lambda i, j: (0,) * a.ndim)
    y, nb, hl = pl.pallas_call(
        _lru_prompt_kernel,
        grid=(b, t // tt),
        in_specs=[row, row, per_seq(CONV_W - 1), per_seq(1)] + [full(w) for w in weights],
        out_specs=[row, per_seq(CONV_W - 1), per_seq(1)],
        out_shape=[jax.ShapeDtypeStruct((b, t, D_C), F32),
                   jax.ShapeDtypeStruct((b, CONV_W - 1, D_C), F32),
                   jax.ShapeDtypeStruct((b, 1, D_C), F32)],
        scratch_shapes=[pltpu.VMEM((tt + 8, D_C), F32), pltpu.VMEM((tt, D_C), F32),
                        pltpu.VMEM((tt, D_C), F32), pltpu.VMEM((tt, D_C), F32),
                        pltpu.VMEM((1, D_C), F32)],
        compiler_params=_params("parallel", "arbitrary"),
        name="lru_prompt",
    )(xr, xg, conv_buf, h0[:, None, :], *weights)
    return y, nb, hl[:, 0, :]


def _lru_sample_kernel(xr_ref, xg_ref, c0_ref, c1_ref, c2_ref, h0_ref, cw_ref, cbias_ref, wa_ref, ba_ref,
                       wx_ref, bx_ref, lam_ref, y_ref, hl_ref):
    xc = (cbias_ref[...] + c0_ref[...] * cw_ref[0:1, :] + c1_ref[...] * cw_ref[1:2, :]
          + c2_ref[...] * cw_ref[2:3, :] + xr_ref[...] * cw_ref[3:4, :])
    a, bterm = _lru_gates(xc, wa_ref, ba_ref, wx_ref, bx_ref, lam_ref)
    h = a * h0_ref[...] + bterm
    hl_ref[...] = h
    y_ref[...] = jax.nn.gelu(xg_ref[...]) * h


def _lru_sample(xr, xg, conv_buf, h0, lp):
    n = xr.shape[0]
    weights = _lru_weights(lp)
    args = [xr, xg, conv_buf[:, 0], conv_buf[:, 1], conv_buf[:, 2], h0] + weights
    full = lambda a: pl.BlockSpec(a.shape, lambda i: (0,) * a.ndim)
    return pl.pallas_call(
        _lru_sample_kernel,
        grid=(1,),
        in_specs=[full(a) for a in args],
        out_specs=[full(xr), full(xr)],
        out_shape=[jax.ShapeDtypeStruct((n, D_C), F32)] * 2,
        compiler_params=_params("arbitrary"),
        name="lru_sample",
    )(*args)


def _outproj_kernel(*refs, merge):
    if merge:
        o1, o2, o3, l1, l2, l3 = refs[:6]
        refs = refs[6:]
        lses = [l1[...], l2[...], l3[...]]
        mm = jnp.maximum(jnp.maximum(lses[0], lses[1]), lses[2])
        ws = [jnp.exp(l - mm) for l in lses]
        ya = (ws[0] * o1[...] + ws[1] * o2[...] + ws[2] * o3[...]) / (ws[0] + ws[1] + ws[2])
    else:
        ya = refs[0][...]
        refs = refs[1:]
    yb_ref, yc_ref, x_ref, g1_ref, sh_ref, sc_ref, og_ref, w_ref, nf_ref, xm_ref, h2_ref = refs
    og = og_ref[...]
    mix = jnp.concatenate([_rms_rows(ya) * og[:, 0:D_A],
                           _rms_rows(yb_ref[...]) * og[:, D_A:D_A + D_B],
                           _rms_rows(yc_ref[...]) * og[:, D_A + D_B:]], axis=-1)
    x = x_ref[...] + g1_ref[...] * jnp.dot(mix.astype(BF16), w_ref[...], preferred_element_type=F32)
    xm_ref[...] = x
    h2 = _rms_rows(x) * nf_ref[...] * (1.0 + sc_ref[...]) + sh_ref[...]
    h2_ref[...] = h2.astype(BF16)


def _outproj(attn, yb, yc, x, g1, sh2, sc2, lp):
    g, t, _ = x.shape
    tm = min(TM_PROJ, t)
    merge = len(attn) > 1
    r = g1.shape[1]
    row = lambda d: pl.BlockSpec((None, tm, d), lambda b, i: (b, i, 0))
    mod_spec = pl.BlockSpec((None, r, D_MODEL), lambda b, i: (b, 0, 0))
    full = lambda a: pl.BlockSpec(a.shape, lambda b, i: (0,) * a.ndim)
    weights = [lp['out_gain'][None, :], lp['w_out'].astype(BF16), lp['norm_ffn'][None, :]]
    return pl.pallas_call(
        functools.partial(_outproj_kernel, merge=merge),
        grid=(g, t // tm),
        in_specs=[row(D_A)] * len(attn) + [row(D_B), row(D_C), row(D_MODEL), mod_spec, mod_spec, mod_spec]
                 + [full(w) for w in weights],
        out_specs=[row(D_MODEL), row(D_MODEL)],
        out_shape=[jax.ShapeDtypeStruct((g, t, D_MODEL), F32), jax.ShapeDtypeStruct((g, t, D_MODEL), BF16)],
        compiler_params=_params("parallel", "parallel"),
        name="outproj_prompt" if merge else "outproj_sample",
    )(*attn, yb, yc, x, g1, sh2, sc2, *weights)


def _sort16_pairs():
    pairs = []

    def merge(lo, hi, r):
        step = r * 2
        if step < hi - lo:
            merge(lo, hi, step)
            merge(lo + r, hi, step)
            pairs.extend((i, i + r) for i in range(lo + r, hi - r, step))
        else:
            pairs.append((lo, lo + r))

    def sort(lo, hi):
        if hi - lo >= 1:
            mid = lo + (hi - lo) // 2
            sort(lo, mid)
            sort(mid + 1, hi)
            merge(lo, hi, 1)

    sort(0, PEER_TOPK - 1)
    return pairs


_SORT16 = _sort16_pairs()
_BITONIC16 = [(i, i + d) for d in (8, 4, 2, 1) for i in range(PEER_TOPK) if not i & d]


def _exchange(vals, pairs):
    vals = list(vals)
    for i, j in pairs:
        hi, lo = jnp.maximum(vals[i], vals[j]), jnp.minimum(vals[i], vals[j])
        vals[i], vals[j] = hi, lo
    return vals


def _merge_top16(a, b):
    return _exchange([jnp.maximum(a[i], b[PEER_TOPK - 1 - i]) for i in range(PEER_TOPK)], _BITONIC16)


def _across_sublanes(vals):
    for shift in (4, 2, 1):
        vals = _merge_top16(vals, [pltpu.roll(v, shift, axis=0) for v in vals])
    return vals


def _top16_of_keys(s):
    slabs = [s[8 * i:8 * i + 8, :] for i in range(N_KEYS // 8)]
    return _across_sublanes(_exchange(slabs, _SORT16))


def _peer_select_kernel(xt_ref, wq_ref, keys_ref, th_ref, e1_ref, s2_ref, e2_ref):
    xt = xt_ref[...]
    tn = xt.shape[1]
    sub = lax.broadcasted_iota(jnp.int32, (8, tn), 0)

    def head(h, carry):
        s = []
        for p in range(2):
            qt = jnp.dot(wq_ref[2 * h + p], xt, preferred_element_type=F32)
            qn = qt * lax.rsqrt(jnp.mean(qt * qt, axis=0, keepdims=True) + EPS)
            s.append(jnp.dot(keys_ref[2 * h + p], qn, precision=HIGHEST, preferred_element_type=F32))
        s1, s2 = s
        v1 = _top16_of_keys(s1)
        v2 = _top16_of_keys(s2)
        a_lo, a_hi = v1[0], v1[8]
        for j in range(1, 8):
            a_lo = jnp.where(sub == j, v1[j], a_lo)
            a_hi = jnp.where(sub == j, v1[8 + j], a_hi)
        sv = _across_sublanes(_merge_top16([a_lo + b for b in v2], [a_hi + b for b in v2]))
        tau = sv[PEER_TOPK - 1]
        z = jnp.ones_like(tau)
        for kk in range(1, PEER_TOPK):
            z = z + jnp.exp(sv[kk] - sv[0])
        inv_z = 1.0 / z
        for i in range(N_KEYS // 8):
            rows = slice(8 * i, 8 * i + 8)
            s1r = s1[rows]
            th = jnp.full((8, tn), BIG, F32)
            for b in range(PEER_TOPK):
                th = jnp.where(s1r + v2[b] >= tau, v2[b], th)
            th_ref[h, rows, :] = th
            e1_ref[h, rows, :] = jnp.exp(s1r - v1[0])
            s2_ref[h, rows, :] = s2[rows]
            e2_ref[h, rows, :] = jnp.exp(s2[rows] - v2[0]) * inv_z
        return carry

    lax.fori_loop(0, PEER_HEADS, head, 0)


def _peer_select(h2t, lp, tn):
    n = h2t.shape[1]
    wq = lp['peer_wq'].T.reshape(2 * PEER_HEADS, D_QUERY // 2, D_MODEL).astype(BF16)
    keys = lp['peer_keys'].reshape(2 * PEER_HEADS, N_KEYS, D_QUERY // 2)
    out = pl.BlockSpec((PEER_HEADS, N_KEYS, tn), lambda i: (0, 0, i))
    return pl.pallas_call(
        _peer_select_kernel,
        grid=(n // tn,),
        in_specs=[pl.BlockSpec((D_MODEL, tn), lambda i: (0, i)),
                  pl.BlockSpec(wq.shape, lambda i: (0, 0, 0)),
                  pl.BlockSpec(keys.shape, lambda i: (0, 0, 0))],
        out_specs=[out] * 4,
        out_shape=[jax.ShapeDtypeStruct((PEER_HEADS, N_KEYS, n), F32)] * 4,
        compiler_params=_params("parallel"),
        name="peer_select",
    )(h2t, wq, keys)


def _peer_dense_kernel(xt_ref, u_ref, vt_ref, th_ref, e1_ref, s2_ref, e2_ref, xm_ref, g2_ref, o_ref,
                       acc, m_s):
    j = pl.program_id(1)

    @pl.when(j == 0)
    def _():
        acc[...] = jnp.zeros_like(acc)

    at = jnp.dot(u_ref[...], xt_ref[...], preferred_element_type=F32)
    for rr in range(u_ref.shape[0] // N_KEYS):
        r = j * (u_ref.shape[0] // N_KEYS) + rr
        rows = slice(rr * N_KEYS, (rr + 1) * N_KEYS)
        gate = jnp.zeros((N_KEYS, at.shape[1]), F32)
        for h in range(PEER_HEADS):
            th = th_ref[h, pl.ds(r, 1), :]
            e1 = e1_ref[h, pl.ds(r, 1), :]
            gate = gate + jnp.where(s2_ref[h] >= th, e2_ref[h], 0.0) * e1
        m_s[rows, :] = (gate * jax.nn.gelu(at[rows])).astype(BF16)
    acc[...] += jnp.dot(vt_ref[...], m_s[...], preferred_element_type=F32)

    @pl.when(j == pl.num_programs(1) - 1)
    def _():
        o_ref[...] = xm_ref[...] + g2_ref[...] * acc[...].T


def _peer_dense(h2t, sel, xmid, g2, lp, tn):
    g, t, _ = xmid.shape
    n = g * t
    te = TE_DENSE
    n_exp = lp['peer_u'].shape[0]
    u = lp['peer_u'].astype(BF16)
    vt = lp['peer_v'].T.astype(BF16)
    tiles_per_seq = t // tn
    r = g2.shape[1]
    tok = pl.BlockSpec((PEER_HEADS, N_KEYS, tn), lambda i, j: (0, 0, i))
    row = pl.BlockSpec((None, tn, D_MODEL), lambda i, j: (i // tiles_per_seq, i % tiles_per_seq, 0))
    return pl.pallas_call(
        _peer_dense_kernel,
        grid=(n // tn, n_exp // te),
        in_specs=[pl.BlockSpec((D_MODEL, tn), lambda i, j: (0, i)),
                  pl.BlockSpec((te, D_MODEL), lambda i, j: (j, 0)),
                  pl.BlockSpec((D_MODEL, te), lambda i, j: (0, j)),
                  tok, tok, tok, tok, row,
                  pl.BlockSpec((None, r, D_MODEL), lambda i, j: (i // tiles_per_seq, 0, 0))],
        out_specs=row,
        out_shape=jax.ShapeDtypeStruct((g, t, D_MODEL), F32),
        scratch_shapes=[pltpu.VMEM((D_MODEL, tn), F32), pltpu.VMEM((te, tn), BF16)],
        compiler_params=_params("parallel", "arbitrary"),
        name="peer_dense",
    )(h2t, u, vt, *sel, xmid, g2)


def _peer(xmid, h2, g2, lp, tn_select, tn_dense):
    g, t, _ = xmid.shape
    h2t = h2.reshape(g * t, D_MODEL).T
    sel = _peer_select(h2t, lp, tn_select)
    return _peer_dense(h2t, sel, xmid, g2, lp, tn_dense)


def _layer_prompt(x, mod, lp, biases):
    sh1, sc1, g1, sh2, sc2, g2 = mod
    b, t, _ = x.shape
    q, k, v, yb, gv, xr, xg = _inproj(x, sh1, sc1, lp, chunked=True)
    branches = [_attn_prompt(q, k, v, bias, dil) for bias, (_, dil) in zip(biases, DILATED_PAIRS)]
    attn = [o for o, _ in branches] + [l for _, l in branches]
    yc, new_buf, h_last = _lru_prompt(xr, xg, jnp.zeros((b, CONV_W - 1, D_C), F32), jnp.zeros((b, D_C), F32), lp)
    xmid, h2 = _outproj(attn, yb, yc, x, g1, sh2, sc2, lp)
    x = _peer(xmid, h2, g2, lp, TN_SELECT, TN_DENSE)
    keep = min(W_MAX, t)
    state = (k[:, t - keep:].reshape(b, keep, H_A, HEAD_DIM), v[:, t - keep:].reshape(b, keep, H_A, HEAD_DIM),
             gv[:, t - CHUNK:], new_buf, h_last)
    return x, state


def _layer_sample(x, mod, lp, rel_bias, cache_k, cache_v, conv_buf, h0, layer):
    sh1, sc1, g1, sh2, sc2, g2 = mod
    _, n, _ = x.shape
    q, k, v, yb, gv, xr, xg = _inproj(x, sh1, sc1, lp, chunked=False)
    ya = _attn_sample(q[0], k[0], v[0], cache_k, cache_v, layer, rel_bias)
    yc, h_last = _lru_sample(xr[0], xg[0], conv_buf, h0, lp)
    xmid, h2 = _outproj([ya[None]], yb, yc[None], x, g1, sh2, sc2, lp)
    x = _peer(xmid, h2, g2, lp, n, n)
    new_buf = jnp.concatenate([conv_buf[:, 1:], xr[0][:, None, :]], axis=1)
    state = (k[0].reshape(n, 1, H_A, HEAD_DIM), v[0].reshape(n, 1, H_A, HEAD_DIM), gv[0][:, None, :],
             new_buf, h_last)
    return x, state


def kernel(x_prompt, x_sample, cache_k, cache_v, state_conv, state_h, c_prompt, c_sample, rel_bias, w_ada, b_ada, norm_mix, norm_ffn, w_in, q_gain, k_gain, gmlp_norm, w_s, b_s, conv_w, conv_b, w_a, b_a, w_x, b_x, lru_lambda, out_gain, w_out, peer_wq, peer_keys, peer_u, peer_v):
    depth = w_in.shape[0]
    nb, ns = x_prompt.shape[0], x_sample.shape[0]
    assert x_sample.shape[1] == 1
    pad = -(nb + ns) % 8
    c_all = jnp.concatenate([c_prompt, c_sample, jnp.zeros((pad, D_MODEL), F32)], axis=0)
    mods = _ada(c_all, w_ada, b_ada)
    biases = [_prompt_bias(rel_bias, window, dil) for window, dil in DILATED_PAIRS]
    xp = x_prompt
    xs = x_sample.reshape(1, ns, D_MODEL)
    st_p, st_s = [], []
    for l in range(depth):
        lp = dict(norm_mix=norm_mix[l], norm_ffn=norm_ffn[l], w_in=w_in[l], q_gain=q_gain[l], k_gain=k_gain[l],
                  gmlp_norm=gmlp_norm[l], w_s=w_s[l], b_s=b_s[l], conv_w=conv_w[l], conv_b=conv_b[l],
                  w_a=w_a[l], b_a=b_a[l], w_x=w_x[l], b_x=b_x[l], lru_lambda=lru_lambda[l],
                  out_gain=out_gain[l], w_out=w_out[l], peer_wq=peer_wq[l], peer_keys=peer_keys[l],
                  peer_u=peer_u[l], peer_v=peer_v[l])
        mod_p = [m[:, None, :] for m in jnp.split(mods[l, :nb], 6, axis=-1)]
        mod_s = [m[None] for m in jnp.split(mods[l, nb:nb + ns], 6, axis=-1)]
        xp, sp = _layer_prompt(xp, mod_p, lp, biases)
        xs, ss = _layer_sample(xs, mod_s, lp, rel_bias, cache_k, cache_v, state_conv[l], state_h[l], l)
        st_p.append(sp)
        st_s.append(ss)
    stack = lambda sts, i: jnp.stack([s[i] for s in sts])
    return (xp, xs.reshape(ns, 1, D_MODEL),
            stack(st_p, 0), stack(st_p, 1), stack(st_s, 0), stack(st_s, 1),
            stack(st_p, 2), stack(st_s, 2), stack(st_p, 3), stack(st_s, 3),
            stack(st_p, 4), stack(st_s, 4))
```

```python
import functools
import math

import jax
import jax.numpy as jnp
import numpy as np
from jax import lax
from jax.experimental import pallas as pl
from jax.experimental.pallas import tpu as pltpu

F32 = jnp.float32
BF16 = jnp.bfloat16
HIGHEST = lax.Precision.HIGHEST

D_MODEL = 1024
HEAD_DIM = 64
D_A = 512
H_A = 8
DILATED_PAIRS = ((128, 1), (512, 4), (2048, 16))
W_MAX = 2048
BLK = 128
ATTN_SCALE = HEAD_DIM ** -0.5
N_BUCKETS = 32
MAX_DIST = 2048
D_B = 256
G_B = 4
E_B = 64
CHUNK = 128
D_C = 256
H_C = 4
E_C = 64
CONV_W = 4
LRU_C = 8.0
PEER_HEADS = 8
N_KEYS = 128
PEER_TOPK = 16
D_QUERY = 256
D_IN = 3 * D_A + 2 * D_B + 2 * D_C
EPS = 1e-6
NEG = -1e30

VMEM_LIMIT_BYTES = 56 * 1024 * 1024

TM_PROJ = 512
TT_LRU = 512
TN_SELECT = 256
TN_DENSE = 512
TE_DENSE = 512
DENSE_SLAB = 16


def _params(*sem):
    return pltpu.CompilerParams(dimension_semantics=sem, vmem_limit_bytes=VMEM_LIMIT_BYTES)


def _split_dot(x, w_bf16):
    hi = x.astype(BF16)
    lo = (x - hi.astype(F32)).astype(BF16)
    return (jnp.dot(hi, w_bf16, preferred_element_type=F32)
            + jnp.dot(lo, w_bf16, preferred_element_type=F32))


def _rms_rows(x):
    return x * lax.rsqrt(jnp.mean(x * x, axis=-1, keepdims=True) + EPS)


def _ada_kernel(c_ref, w_ref, b_ref, o_ref):
    c = c_ref[...]
    o_ref[...] = jnp.dot(jax.nn.silu(c), w_ref[...], precision=HIGHEST,
                         preferred_element_type=F32) + b_ref[...]


def _ada(c_all, w_ada, b_ada):
    depth, _, n_out = w_ada.shape
    rows = c_all.shape[0]
    tn = 512
    return pl.pallas_call(
        _ada_kernel,
        grid=(depth, n_out // tn),
        in_specs=[pl.BlockSpec((rows, D_MODEL), lambda l, j: (0, 0)),
                  pl.BlockSpec((None, D_MODEL, tn), lambda l, j: (l, 0, j)),
                  pl.BlockSpec((None, 1, tn), lambda l, j: (l, 0, j))],
        out_specs=pl.BlockSpec((None, rows, tn), lambda l, j: (l, 0, j)),
        out_shape=jax.ShapeDtypeStruct((depth, rows, n_out), F32),
        compiler_params=_params("parallel", "parallel"),
        name="ada",
    )(c_all, w_ada, b_ada.reshape(depth, 1, n_out))


def _inproj_kernel(x_ref, sh_ref, sc_ref, nm_ref, w_ref, qg_ref, kg_ref, gn_ref, ones_ref,
                   gw_ref, gb_ref, q_ref, k_ref, v_ref, yb_ref, gv_ref, xr_ref, xg_ref, *, chunked):
    x = x_ref[...]
    h = _rms_rows(x) * nm_ref[...] * (1.0 + sc_ref[...]) + sh_ref[...]
    z = jnp.dot(h.astype(BF16), w_ref[...], preferred_element_type=F32)
    ones = ones_ref[...]

    def headnorm(a, g):
        ss = _split_dot(a * a, ones)
        return a * lax.rsqrt(ss * (1.0 / HEAD_DIM) + EPS) * g

    q_ref[...] = headnorm(z[:, 0:D_A], qg_ref[...]) * ATTN_SCALE
    k_ref[...] = headnorm(z[:, D_A:2 * D_A], kg_ref[...])
    v_ref[...] = z[:, 2 * D_A:3 * D_A]
    o = 3 * D_A
    ub = z[:, o:o + D_B]
    vbn = _rms_rows(z[:, o + D_B:o + 2 * D_B]) * gn_ref[...]
    gv_ref[...] = vbn
    xr_ref[...] = z[:, o + 2 * D_B:o + 2 * D_B + D_C]
    xg_ref[...] = z[:, o + 2 * D_B + D_C:o + 2 * D_B + 2 * D_C]
    if chunked:
        rg = lax.broadcasted_iota(jnp.int32, (G_B * CHUNK, D_B), 0) // CHUNK
        cg = lax.broadcasted_iota(jnp.int32, (G_B * CHUNK, D_B), 1) // E_B
        keep = rg == cg
        for ci in range(x.shape[0] // CHUNK):
            rows = slice(ci * CHUNK, (ci + 1) * CHUNK)
            vc = vbn[rows]
            vbd = jnp.where(keep, jnp.concatenate([vc] * G_B, axis=0), 0.0)
            mix = _split_dot_rhs(gw_ref[...], vbd) + gb_ref[...]
            yb_ref[rows, :] = ub[rows] * mix
    else:
        yb_ref[...] = ub * (gw_ref[...] * vbn + gb_ref[...])


def _split_dot_rhs(w_bf16_pair, x):
    w_hi, w_lo = w_bf16_pair[0], w_bf16_pair[1]
    x_hi = x.astype(BF16)
    x_lo = (x - x_hi.astype(F32)).astype(BF16)
    return (jnp.dot(w_hi, x_hi, preferred_element_type=F32)
            + jnp.dot(w_hi, x_lo, preferred_element_type=F32)
            + jnp.dot(w_lo, x_hi, preferred_element_type=F32))


def _bf16_pair(w):
    hi = w.astype(BF16)
    lo = (w - hi.astype(F32)).astype(BF16)
    return jnp.stack([hi, lo])


def _inproj(x, sh, sc, lp, *, chunked):
    g, t, _ = x.shape
    tm = min(TM_PROJ, t)
    r = sh.shape[1]
    mod_spec = pl.BlockSpec((None, r, D_MODEL), (lambda b, i: (b, 0, 0)))
    full = lambda a: pl.BlockSpec(a.shape, lambda b, i: (0,) * a.ndim)
    if chunked:
        gw = _bf16_pair(jnp.transpose(jnp.tril(lp['w_s']), (1, 0, 2)).reshape(CHUNK, G_B * CHUNK))
        gb = jnp.repeat(lp['b_s'].T, E_B, axis=1)
    else:
        gw = jnp.repeat(lp['w_s'][:, 0, 0], E_B)[None, :]
        gb = jnp.repeat(lp['b_s'][:, 0], E_B)[None, :]
    weights = [lp['norm_mix'][None, :], lp['w_in'].astype(BF16),
               jnp.tile(lp['q_gain'], H_A)[None, :], jnp.tile(lp['k_gain'], H_A)[None, :],
               lp['gmlp_norm'][None, :], _head_ones(), gw, gb]
    row = lambda d: pl.BlockSpec((None, tm, d), lambda b, i: (b, i, 0))
    widths = (D_A, D_A, D_A, D_B, D_B, D_C, D_C)
    return pl.pallas_call(
        functools.partial(_inproj_kernel, chunked=chunked),
        grid=(g, t // tm),
        in_specs=[row(D_MODEL), mod_spec, mod_spec] + [full(w) for w in weights],
        out_specs=[row(d) for d in widths],
        out_shape=[jax.ShapeDtypeStruct((g, t, d), F32) for d in widths],
        compiler_params=_params("parallel", "parallel"),
        name="inproj_prompt" if chunked else "inproj_sample",
    )(x, sh, sc, *weights)


def _head_ones():
    hid = np.arange(D_A) // HEAD_DIM
    return jnp.asarray((hid[:, None] == hid[None, :]).astype(np.float32), dtype=BF16)


def _t5_bucket(dist):
    exact = N_BUCKETS // 2
    df = np.maximum(dist, 1).astype(np.float64)
    large = exact + (np.log(df / exact) / math.log(MAX_DIST / exact) * (N_BUCKETS - exact)).astype(np.int64)
    return np.where(dist < exact, dist, np.minimum(large, N_BUCKETS - 1))


def _bias_lookup(rel_bias, dist):
    onehot = (_t5_bucket(dist)[..., None] == np.arange(N_BUCKETS)).astype(np.float32)
    return jnp.einsum('...b,bh->...h', jnp.asarray(onehot), rel_bias.astype(F32), precision=HIGHEST)


def _prompt_bias(rel_bias, window, dil):
    qi = np.arange(BLK)[:, None] + BLK
    ki = np.arange(2 * BLK)[None, :]
    dm = qi - ki
    valid = (dm >= 0) & (dm <= window // dil)
    bias = jnp.transpose(_bias_lookup(rel_bias, np.maximum(dm, 0) * dil), (2, 0, 1))
    return jnp.where(jnp.asarray(valid)[None], bias, NEG)


def _attn_prompt_kernel(q_ref, kp_ref, kc_ref, vp_ref, vc_ref, bias_ref, o_ref, lse_ref):
    first = pl.program_id(2) == 0
    q = q_ref[...].astype(BF16)
    k2 = jnp.concatenate([kp_ref[...], kc_ref[...]], axis=0).astype(BF16)
    v2 = jnp.concatenate([vp_ref[...], vc_ref[...]], axis=0).astype(BF16)
    col = lax.broadcasted_iota(jnp.int32, (BLK, 2 * BLK), 1)
    no_prev = jnp.logical_and(first, col < BLK)
    for h in range(H_A):
        hs = slice(h * HEAD_DIM, (h + 1) * HEAD_DIM)
        lg = lax.dot_general(q[:, hs], k2[:, hs], (((1,), (1,)), ((), ())),
                             preferred_element_type=F32) + bias_ref[h]
        lg = jnp.where(no_prev, NEG, lg)
        m = jnp.max(lg, axis=-1, keepdims=True)
        p = jnp.exp(lg - m)
        s = jnp.sum(p, axis=-1, keepdims=True)
        o = jnp.dot(p.astype(BF16), v2[:, hs], preferred_element_type=F32) / s
        o_ref[:, hs] = o
        lse_ref[:, hs] = jnp.broadcast_to(m + jnp.log(s), (BLK, HEAD_DIM))


def _attn_prompt(q, k, v, bias, dil):
    b, s, _ = q.shape
    length = s // dil
    nb = length // BLK
    view = lambda a: a.reshape(b, length, dil * D_A)
    cur = pl.BlockSpec((None, BLK, D_A), lambda i, p, n: (i, n, p))
    prev = pl.BlockSpec((None, BLK, D_A), lambda i, p, n: (i, jnp.maximum(n - 1, 0), p))
    o, lse = pl.pallas_call(
        _attn_prompt_kernel,
        grid=(b, dil, nb),
        in_specs=[cur, prev, cur, prev, cur,
                  pl.BlockSpec(bias.shape, lambda i, p, n: (0, 0, 0))],
        out_specs=[cur, cur],
        out_shape=[jax.ShapeDtypeStruct((b, length, dil * D_A), F32)] * 2,
        compiler_params=_params("parallel", "parallel", "arbitrary"),
        name=f"attn_prompt_d{dil}",
    )(view(q), view(k), view(k), view(v), view(v), bias)
    return o.reshape(b, s, D_A), lse.reshape(b, s, D_A)


def _attn_sample_kernel(qt_ref, knt_ref, vnt_ref, b0_ref, kt_ref, vt_ref, br1_ref, br4_ref, br16_ref, ot_ref):
    qt = qt_ref[0]
    vnt = vnt_ref[0]
    w_cache = kt_ref.shape[-1]
    l0 = jnp.sum(qt * knt_ref[0], axis=0, keepdims=True) + b0_ref[...]
    for h in range(H_A):
        col = slice(h, h + 1)
        logits = jnp.sum(kt_ref[0, h] * qt[:, col], axis=0, keepdims=True)
        l0h = l0[:, col]
        outs, lses = [], []
        for br_ref in (br1_ref, br4_ref, br16_ref):
            span = br_ref.shape[-1]
            lg = logits[:, w_cache - span:] + br_ref[h]
            m = jnp.maximum(jnp.max(lg, axis=1, keepdims=True), l0h)
            p = jnp.exp(lg - m)
            p0 = jnp.exp(l0h - m)
            s = jnp.sum(p, axis=1, keepdims=True) + p0
            pv = jnp.sum(vt_ref[0, h, :, w_cache - span:] * p, axis=1, keepdims=True)
            outs.append((pv + p0 * vnt[:, col]) / s)
            lses.append(m + jnp.log(s))
        mm = jnp.maximum(jnp.maximum(lses[0], lses[1]), lses[2])
        ws = [jnp.exp(l - mm) for l in lses]
        ot_ref[0, :, col] = (ws[0] * outs[0] + ws[1] * outs[1] + ws[2] * outs[2]) / (ws[0] + ws[1] + ws[2])


def _attn_sample(q, kn, vn, cache_k, cache_v, layer, rel_bias):
    nseq = q.shape[0]
    w_cache = cache_k.shape[2]
    cols = lambda a: jnp.transpose(a.reshape(nseq, H_A, HEAD_DIM), (0, 2, 1))
    brs = []
    for window, dil in DILATED_PAIRS:
        assert window // dil == BLK and w_cache >= window
        dist = window - np.arange(window)
        rows = jnp.where(jnp.asarray(dist % dil == 0)[None], _bias_lookup(rel_bias, dist).T, NEG)
        brs.append(rows[:, None, :])
    b0 = rel_bias[0].astype(F32)[None, :]
    col = pl.BlockSpec((1, HEAD_DIM, H_A), lambda i: (i, 0, 0))
    cache = pl.BlockSpec((None, 1, H_A, HEAD_DIM, w_cache), lambda i: (layer, i, 0, 0, 0))
    full = lambda a: pl.BlockSpec(a.shape, lambda i: (0,) * a.ndim)
    out = pl.pallas_call(
        _attn_sample_kernel,
        grid=(nseq,),
        in_specs=[col, col, col, full(b0), cache, cache] + [full(a) for a in brs],
        out_specs=col,
        out_shape=jax.ShapeDtypeStruct((nseq, HEAD_DIM, H_A), F32),
        compiler_params=_params("parallel"),
        name="attn_sample",
    )(cols(q), cols(kn), cols(vn), b0,
      jnp.transpose(cache_k, (0, 1, 3, 4, 2)), jnp.transpose(cache_v, (0, 1, 3, 4, 2)), *brs)
    return jnp.transpose(out, (0, 2, 1)).reshape(nseq, D_A)


def _lru_gates(xc, wa_ref, ba_ref, wx_ref, bx_ref, lam_ref):
    r = jax.nn.sigmoid(jnp.dot(xc, wa_ref[...], precision=HIGHEST, preferred_element_type=F32) + ba_ref[...])
    ig = jax.nn.sigmoid(jnp.dot(xc, wx_ref[...], precision=HIGHEST, preferred_element_type=F32) + bx_ref[...])
    nl = -lam_ref[...]
    softplus = jnp.maximum(nl, 0.0) + jnp.log(1.0 + jnp.exp(-jnp.abs(nl)))
    log_a = -LRU_C * r * softplus
    a = jnp.exp(log_a)
    bterm = jnp.sqrt(1.0 - a * a) * ig * xc
    return a, bterm


def _lru_prompt_kernel(xr_ref, xg_ref, cb_ref, h0_ref, cw_ref, cbias_ref, wa_ref, ba_ref, wx_ref, bx_ref,
                       lam_ref, y_ref, nb_ref, hl_ref, xbuf, a_s, b_s, h_s, hcar):
    tt = xr_ref.shape[0]
    pad = 8

    @pl.when(pl.program_id(1) == 0)
    def _():
        xbuf[0:pad, :] = jnp.zeros((pad, D_C), F32)
        xbuf[pad - (CONV_W - 1):pad, :] = cb_ref[...]
        hcar[...] = h0_ref[...]

    xbuf[pad:pad + tt, :] = xr_ref[...]
    xc = cbias_ref[...]
    for kk in range(CONV_W):
        off = pad - (CONV_W - 1) + kk
        xc = xc + xbuf[off:off + tt, :] * cw_ref[kk:kk + 1, :]
    a, bterm = _lru_gates(xc, wa_ref, ba_ref, wx_ref, bx_ref, lam_ref)
    a_s[...] = a
    b_s[...] = bterm

    def step(i, h):
        h = a_s[pl.ds(i, 1), :] * h + b_s[pl.ds(i, 1), :]
        h_s[pl.ds(i, 1), :] = h
        return h

    h = lax.fori_loop(0, tt, step, hcar[...], unroll=8)
    hcar[...] = h
    hl_ref[...] = h
    y_ref[...] = jax.nn.gelu(xg_ref[...]) * h_s[...]
    tail = xbuf[pad + tt - (CONV_W - 1):pad + tt, :]
    nb_ref[...] = tail
    xbuf[pad - (CONV_W - 1):pad, :] = tail


def _lru_weights(lp):
    def blockdiag(w):
        eye = jnp.eye(H_C, dtype=w.dtype)
        return jnp.einsum('hij,hg->higj', w, eye).reshape(D_C, D_C)
    return [lp['conv_w'], lp['conv_b'][None, :], blockdiag(lp['w_a']), lp['b_a'][None, :],
            blockdiag(lp['w_x']), lp['b_x'][None, :], lp['lru_lambda'][None, :]]


def _lru_prompt(xr, xg, conv_buf, h0, lp):
    b, t, _ = xr.shape
    tt = min(TT_LRU, t)
    assert t % tt == 0 and tt >= CONV_W - 1
    weights = _lru_weights(lp)
    row = pl.BlockSpec((None, tt, D_C), lambda i, j: (i, j, 0))
    per_seq = lambda n: pl.BlockSpec((None, n, D_C), lambda i, j: (i, 0, 0))
    full = lambda a: pl.BlockSpec(a.shape, lambda i, j: (0,) * a.ndim)
    y, nb, hl = pl.pallas_call(
        _lru_prompt_kernel,
        grid=(b, t // tt),
        in_specs=[row, row, per_seq(CONV_W - 1), per_seq(1)] + [full(w) for w in weights],
        out_specs=[row, per_seq(CONV_W - 1), per_seq(1)],
        out_shape=[jax.ShapeDtypeStruct((b, t, D_C), F32),
                   jax.ShapeDtypeStruct((b, CONV_W - 1, D_C), F32),
                   jax.ShapeDtypeStruct((b, 1, D_C), F32)],
        scratch_shapes=[pltpu.VMEM((tt + 8, D_C), F32), pltpu.VMEM((tt, D_C), F32),
                        pltpu.VMEM((tt, D_C), F32), pltpu.VMEM((tt, D_C), F32),
                        pltpu.VMEM((1, D_C), F32)],
        compiler_params=_params("parallel", "arbitrary"),
        name="lru_prompt",
    )(xr, xg, conv_buf, h0[:, None, :], *weights)
    return y, nb, hl[:, 0, :]


def _lru_sample_kernel(xr_ref, xg_ref, c0_ref, c1_ref, c2_ref, h0_ref, cw_ref, cbias_ref, wa_ref, ba_ref,
                       wx_ref, bx_ref, lam_ref, y_ref, hl_ref):
    xc = (cbias_ref[...] + c0_ref[...] * cw_ref[0:1, :] + c1_ref[...] * cw_ref[1:2, :]
          + c2_ref[...] * cw_ref[2:3, :] + xr_ref[...] * cw_ref[3:4, :])
    a, bterm = _lru_gates(xc, wa_ref, ba_ref, wx_ref, bx_ref, lam_ref)
    h = a * h0_ref[...] + bterm
    hl_ref[...] = h
    y_ref[...] = jax.nn.gelu(xg_ref[...]) * h


def _lru_sample(xr, xg, conv_buf, h0, lp):
    n = xr.shape[0]
    weights = _lru_weights(lp)
    args = [xr, xg, conv_buf[:, 0], conv_buf[:, 1], conv_buf[:, 2], h0] + weights
    full = lambda a: pl.BlockSpec(a.shape, lambda i: (0,) * a.ndim)
    return pl.pallas_call(
        _lru_sample_kernel,
        grid=(1,),
        in_specs=[full(a) for a in args],
        out_specs=[full(xr), full(xr)],
        out_shape=[jax.ShapeDtypeStruct((n, D_C), F32)] * 2,
        compiler_params=_params("arbitrary"),
        name="lru_sample",
    )(*args)


def _outproj_kernel(*refs, merge):
    if merge:
        o1, o2, o3, l1, l2, l3 = refs[:6]
        refs = refs[6:]
        lses = [l1[...], l2[...], l3[...]]
        mm = jnp.maximum(jnp.maximum(lses[0], lses[1]), lses[2])
        ws = [jnp.exp(l - mm) for l in lses]
        ya = (ws[0] * o1[...] + ws[1] * o2[...] + ws[2] * o3[...]) / (ws[0] + ws[1] + ws[2])
    else:
        ya = refs[0][...]
        refs = refs[1:]
    yb_ref, yc_ref, x_ref, g1_ref, sh_ref, sc_ref, og_ref, w_ref, nf_ref, xm_ref, h2_ref = refs
    og = og_ref[...]
    mix = jnp.concatenate([_rms_rows(ya) * og[:, 0:D_A],
                           _rms_rows(yb_ref[...]) * og[:, D_A:D_A + D_B],
                           _rms_rows(yc_ref[...]) * og[:, D_A + D_B:]], axis=-1)
    x = x_ref[...] + g1_ref[...] * jnp.dot(mix.astype(BF16), w_ref[...], preferred_element_type=F32)
    xm_ref[...] = x
    h2 = _rms_rows(x) * nf_ref[...] * (1.0 + sc_ref[...]) + sh_ref[...]
    h2_ref[...] = h2.astype(BF16)


def _outproj(attn, yb, yc, x, g1, sh2, sc2, lp):
    g, t, _ = x.shape
    tm = min(TM_PROJ, t)
    merge = len(attn) > 1
    r = g1.shape[1]
    row = lambda d: pl.BlockSpec((None, tm, d), lambda b, i: (b, i, 0))
    mod_spec = pl.BlockSpec((None, r, D_MODEL), lambda b, i: (b, 0, 0))
    full = lambda a: pl.BlockSpec(a.shape, lambda b, i: (0,) * a.ndim)
    weights = [lp['out_gain'][None, :], lp['w_out'].astype(BF16), lp['norm_ffn'][None, :]]
    return pl.pallas_call(
        functools.partial(_outproj_kernel, merge=merge),
        grid=(g, t // tm),
        in_specs=[row(D_A)] * len(attn) + [row(D_B), row(D_C), row(D_MODEL), mod_spec, mod_spec, mod_spec]
                 + [full(w) for w in weights],
        out_specs=[row(D_MODEL), row(D_MODEL)],
        out_shape=[jax.ShapeDtypeStruct((g, t, D_MODEL), F32), jax.ShapeDtypeStruct((g, t, D_MODEL), BF16)],
        compiler_params=_params("parallel", "parallel"),
        name="outproj_prompt" if merge else "outproj_sample",
    )(*attn, yb, yc, x, g1, sh2, sc2, *weights)


def _sort16_pairs():
    pairs = []

    def merge(lo, hi, r):
        step = r * 2
        if step < hi - lo:
            merge(lo, hi, step)
            merge(lo + r, hi, step)
            pairs.extend((i, i + r) for i in range(lo + r, hi - r, step))
        else:
            pairs.append((lo, lo + r))

    def sort(lo, hi):
        if hi - lo >= 1:
            mid = lo + (hi - lo) // 2
            sort(lo, mid)
            sort(mid + 1, hi)
            merge(lo, hi, 1)

    sort(0, PEER_TOPK - 1)
    return pairs


_SORT16 = _sort16_pairs()
_BITONIC16 = [(i, i + d) for d in (8, 4, 2, 1) for i in range(PEER_TOPK) if not i & d]


def _exchange(vals, pairs):
    vals = list(vals)
    for i, j in pairs:
        hi, lo = jnp.maximum(vals[i], vals[j]), jnp.minimum(vals[i], vals[j])
        vals[i], vals[j] = hi, lo
    return vals


def _merge_top16(a, b):
    return _exchange([jnp.maximum(a[i], b[PEER_TOPK - 1 - i]) for i in range(PEER_TOPK)], _BITONIC16)


def _across_sublanes(vals):
    for shift in (4, 2, 1):
        vals = _merge_top16(vals, [pltpu.roll(v, shift, axis=0) for v in vals])
    return vals


def _top16_of_keys(s):
    slabs = [s[8 * i:8 * i + 8, :] for i in range(N_KEYS // 8)]
    return _across_sublanes(_exchange(slabs, _SORT16))


def _peer_select_kernel(xt_ref, wq_ref, keys_ref, cnt_ref, e1_ref, rk_ref, e2_ref):
    xt = xt_ref[...]
    tn = xt.shape[1]
    sub = lax.broadcasted_iota(jnp.int32, (8, tn), 0)

    def head(h, carry):
        s = []
        for p in range(2):
            qt = jnp.dot(wq_ref[2 * h + p], xt, preferred_element_type=F32)
            qn = qt * lax.rsqrt(jnp.mean(qt * qt, axis=0, keepdims=True) + EPS)
            s.append(jnp.dot(keys_ref[2 * h + p], qn, precision=HIGHEST, preferred_element_type=F32))
        s1, s2 = s
        v1 = _top16_of_keys(s1)
        v2 = _top16_of_keys(s2)
        a_lo, a_hi = v1[0], v1[8]
        for j in range(1, 8):
            a_lo = jnp.where(sub == j, v1[j], a_lo)
            a_hi = jnp.where(sub == j, v1[8 + j], a_hi)
        sv = _across_sublanes(_merge_top16([a_lo + b for b in v2], [a_hi + b for b in v2]))
        tau = sv[PEER_TOPK - 1][None]
        z = jnp.ones_like(sv[0])
        for kk in range(1, PEER_TOPK):
            z = z + jnp.exp(sv[kk] - sv[0])
        inv_z = 1.0 / z
        for i in range(N_KEYS // 16):
            rows = slice(16 * i, 16 * i + 16)
            s1r = s1[rows].reshape(2, 8, tn)
            s2r = s2[rows].reshape(2, 8, tn)
            cnt = jnp.zeros((2, 8, tn), F32)
            rk = jnp.zeros((2, 8, tn), F32)
            for b in range(PEER_TOPK):
                vb = v2[b][None]
                cnt = cnt + jnp.where(s1r + vb >= tau, 1.0, 0.0)
                rk = rk + jnp.where(vb > s2r, 1.0, 0.0)
            cnt_ref[h, rows, :] = cnt.reshape(16, tn)
            e1_ref[h, rows, :] = jnp.exp(s1r - v1[0][None]).reshape(16, tn)
            rk_ref[h, rows, :] = rk.reshape(16, tn).astype(BF16)
            e2_ref[h, rows, :] = (jnp.exp(s2r - v2[0][None]) * inv_z[None]).reshape(16, tn).astype(BF16)
        return carry

    lax.fori_loop(0, PEER_HEADS, head, 0)


def _peer_select(h2t, lp, tn):
    n = h2t.shape[1]
    wq = lp['peer_wq'].T.reshape(2 * PEER_HEADS, D_QUERY // 2, D_MODEL).astype(BF16)
    keys = lp['peer_keys'].reshape(2 * PEER_HEADS, N_KEYS, D_QUERY // 2)
    out = pl.BlockSpec((PEER_HEADS, N_KEYS, tn), lambda i: (0, 0, i))
    shape = lambda dt: jax.ShapeDtypeStruct((PEER_HEADS, N_KEYS, n), dt)
    return pl.pallas_call(
        _peer_select_kernel,
        grid=(n // tn,),
        in_specs=[pl.BlockSpec((D_MODEL, tn), lambda i: (0, i)),
                  pl.BlockSpec(wq.shape, lambda i: (0, 0, 0)),
                  pl.BlockSpec(keys.shape, lambda i: (0, 0, 0))],
        out_specs=[out] * 4,
        out_shape=[shape(F32), shape(F32), shape(BF16), shape(BF16)],
        compiler_params=_params("parallel"),
        name="peer_select",
    )(h2t, wq, keys)


def _peer_dense_kernel(xt_ref, u0_ref, u_ref, vt_ref, cnt_ref, e1_ref, rk_ref, e2_ref, xm_ref, g2_ref, o_ref,
                       acc, at_s, m_s, cntb, e1b):
    j = pl.program_id(1)
    last = pl.num_programs(1) - 1
    te, tn = at_s.shape
    n_r = te // N_KEYS
    slab = DENSE_SLAB

    @pl.when(j == 0)
    def _():
        acc[...] = jnp.zeros_like(acc)
        m_s[...] = jnp.zeros_like(m_s)
        at_s[...] = jnp.dot(u0_ref[...], xt_ref[...], preferred_element_type=F32)

    acc[...] += jnp.dot(vt_ref[...], m_s[...], preferred_element_type=F32)

    r0 = jnp.minimum(j, last - 1) * n_r
    for rr in range(n_r):
        for h in range(PEER_HEADS):
            cntb[rr * PEER_HEADS + h] = jnp.broadcast_to(cnt_ref[h, pl.ds(r0 + rr, 1), :], (slab, tn)).astype(BF16)
            e1b[rr * PEER_HEADS + h] = jnp.broadcast_to(e1_ref[h, pl.ds(r0 + rr, 1), :], (slab, tn)).astype(BF16)
    for s in range(te // slab):
        rr = s * slab // N_KEYS
        keys2 = slice(s * slab % N_KEYS, s * slab % N_KEYS + slab)
        experts = slice(s * slab, (s + 1) * slab)
        gate = jnp.zeros((slab, tn), BF16)
        for h in range(PEER_HEADS):
            picked = rk_ref[h, keys2, :] < cntb[rr * PEER_HEADS + h]
            gate = gate + jnp.where(picked, e2_ref[h, keys2, :], 0.0) * e1b[rr * PEER_HEADS + h]
        m_s[experts, :] = gate * jax.nn.gelu(at_s[experts, :]).astype(BF16)

    at_s[...] = jnp.dot(u_ref[...], xt_ref[...], preferred_element_type=F32)

    @pl.when(j == last)
    def _():
        o_ref[...] = xm_ref[...] + g2_ref[...] * acc[...].T


def _peer_dense(h2t, sel, xmid, g2, lp, tn):
    g, t, _ = xmid.shape
    n = g * t
    te = TE_DENSE
    n_exp = lp['peer_u'].shape[0]
    nj = n_exp // te
    u = lp['peer_u'].astype(BF16)
    vt = lp['peer_v'].T.astype(BF16)
    tiles_per_seq = t // tn
    r = g2.shape[1]
    tok = pl.BlockSpec((PEER_HEADS, N_KEYS, tn), lambda i, j: (0, 0, i))
    row = pl.BlockSpec((None, tn, D_MODEL), lambda i, j: (i // tiles_per_seq, i % tiles_per_seq, 0))
    return pl.pallas_call(
        _peer_dense_kernel,
        grid=(n // tn, nj + 1),
        in_specs=[pl.BlockSpec((D_MODEL, tn), lambda i, j: (0, i)),
                  pl.BlockSpec((te, D_MODEL), lambda i, j: (0, 0)),
                  pl.BlockSpec((te, D_MODEL), lambda i, j: (jnp.minimum(j + 1, nj - 1), 0)),
                  pl.BlockSpec((D_MODEL, te), lambda i, j: (0, jnp.maximum(j - 1, 0))),
                  tok, tok, tok, tok, row,
                  pl.BlockSpec((None, r, D_MODEL), lambda i, j: (i // tiles_per_seq, 0, 0))],
        out_specs=row,
        out_shape=jax.ShapeDtypeStruct((g, t, D_MODEL), F32),
        scratch_shapes=[pltpu.VMEM((D_MODEL, tn), F32), pltpu.VMEM((te, tn), F32), pltpu.VMEM((te, tn), BF16),
                        pltpu.VMEM((te // N_KEYS * PEER_HEADS, DENSE_SLAB, tn), BF16),
                        pltpu.VMEM((te // N_KEYS * PEER_HEADS, DENSE_SLAB, tn), BF16)],
        compiler_params=_params("parallel", "arbitrary"),
        name="peer_dense",
    )(h2t, u, u, vt, *sel, xmid, g2)


def _peer(xmid, h2, g2, lp, tn_select, tn_dense):
    g, t, _ = xmid.shape
    h2t = h2.reshape(g * t, D_MODEL).T
    sel = _peer_select(h2t, lp, tn_select)
    return _peer_dense(h2t, sel, xmid, g2, lp, tn_dense)


def _layer_prompt(x, mod, lp, biases):
    sh1, sc1, g1, sh2, sc2, g2 = mod
    b, t, _ = x.shape
    q, k, v, yb, gv, xr, xg = _inproj(x, sh1, sc1, lp, chunked=True)
    branches = [_attn_prompt(q, k, v, bias, dil) for bias, (_, dil) in zip(biases, DILATED_PAIRS)]
    attn = [o for o, _ in branches] + [l for _, l in branches]
    yc, new_buf, h_last = _lru_prompt(xr, xg, jnp.zeros((b, CONV_W - 1, D_C), F32), jnp.zeros((b, D_C), F32), lp)
    xmid, h2 = _outproj(attn, yb, yc, x, g1, sh2, sc2, lp)
    x = _peer(xmid, h2, g2, lp, TN_SELECT, TN_DENSE)
    keep = min(W_MAX, t)
    state = (k[:, t - keep:].reshape(b, keep, H_A, HEAD_DIM), v[:, t - keep:].reshape(b, keep, H_A, HEAD_DIM),
             gv[:, t - CHUNK:], new_buf, h_last)
    return x, state


def _layer_sample(x, mod, lp, rel_bias, cache_k, cache_v, conv_buf, h0, layer):
    sh1, sc1, g1, sh2, sc2, g2 = mod
    _, n, _ = x.shape
    q, k, v, yb, gv, xr, xg = _inproj(x, sh1, sc1, lp, chunked=False)
    ya = _attn_sample(q[0], k[0], v[0], cache_k, cache_v, layer, rel_bias)
    yc, h_last = _lru_sample(xr[0], xg[0], conv_buf, h0, lp)
    xmid, h2 = _outproj([ya[None]], yb, yc[None], x, g1, sh2, sc2, lp)
    x = _peer(xmid, h2, g2, lp, n, n)
    new_buf = jnp.concatenate([conv_buf[:, 1:], xr[0][:, None, :]], axis=1)
    state = (k[0].reshape(n, 1, H_A, HEAD_DIM), v[0].reshape(n, 1, H_A, HEAD_DIM), gv[0][:, None, :],
             new_buf, h_last)
    return x, state


def kernel(x_prompt, x_sample, cache_k, cache_v, state_conv, state_h, c_prompt, c_sample, rel_bias, w_ada, b_ada, norm_mix, norm_ffn, w_in, q_gain, k_gain, gmlp_norm, w_s, b_s, conv_w, conv_b, w_a, b_a, w_x, b_x, lru_lambda, out_gain, w_out, peer_wq, peer_keys, peer_u, peer_v):
    depth = w_in.shape[0]
    nb, ns = x_prompt.shape[0], x_sample.shape[0]
    assert x_sample.shape[1] == 1
    pad = -(nb + ns) % 8
    c_all = jnp.concatenate([c_prompt, c_sample, jnp.zeros((pad, D_MODEL), F32)], axis=0)
    mods = _ada(c_all, w_ada, b_ada)
    biases = [_prompt_bias(rel_bias, window, dil) for window, dil in DILATED_PAIRS]
    xp = x_prompt
    xs = x_sample.reshape(1, ns, D_MODEL)
    st_p, st_s = [], []
    for l in range(depth):
        lp = dict(norm_mix=norm_mix[l], norm_ffn=norm_ffn[l], w_in=w_in[l], q_gain=q_gain[l], k_gain=k_gain[l],
                  gmlp_norm=gmlp_norm[l], w_s=w_s[l], b_s=b_s[l], conv_w=conv_w[l], conv_b=conv_b[l],
                  w_a=w_a[l], b_a=b_a[l], w_x=w_x[l], b_x=b_x[l], lru_lambda=lru_lambda[l],
                  out_gain=out_gain[l], w_out=w_out[l], peer_wq=peer_wq[l], peer_keys=peer_keys[l],
                  peer_u=peer_u[l], peer_v=peer_v[l])
        mod_p = [m[:, None, :] for m in jnp.split(mods[l, :nb], 6, axis=-1)]
        mod_s = [m[None] for m in jnp.split(mods[l, nb:nb + ns], 6, axis=-1)]
        xp, sp = _layer_prompt(xp, mod_p, lp, biases)
        xs, ss = _layer_sample(xs, mod_s, lp, rel_bias, cache_k, cache_v, state_conv[l], state_h[l], l)
        st_p.append(sp)
        st_s.append(ss)
    stack = lambda sts, i: jnp.stack([s[i] for s in sts])
    return (xp, xs.reshape(ns, 1, D_MODEL),
            stack(st_p, 0), stack(st_p, 1), stack(st_s, 0), stack(st_s, 1),
            stack(st_p, 2), stack(st_s, 2), stack(st_p, 3), stack(st_s, 3),
            stack(st_p, 4), stack(st_s, 4))
```

```python
import functools
import math

import jax
import jax.numpy as jnp
import numpy as np
from jax import lax
from jax.experimental import pallas as pl
from jax.experimental.pallas import tpu as pltpu

F32 = jnp.float32
BF16 = jnp.bfloat16
HIGHEST = lax.Precision.HIGHEST

D_MODEL = 1024
HEAD_DIM = 64
D_A = 512
H_A = 8
DILATED_PAIRS = ((128, 1), (512, 4), (2048, 16))
W_MAX = 2048
BLK = 128
ATTN_SCALE = HEAD_DIM ** -0.5
N_BUCKETS = 32
MAX_DIST = 2048
D_B = 256
G_B = 4
E_B = 64
CHUNK = 128
D_C = 256
H_C = 4
E_C = 64
CONV_W = 4
LRU_C = 8.0
PEER_HEADS = 8
N_KEYS = 128
PEER_TOPK = 16
D_QUERY = 256
D_IN = 3 * D_A + 2 * D_B + 2 * D_C
EPS = 1e-6
NEG = -1e30

VMEM_LIMIT_BYTES = 56 * 1024 * 1024

TM_PROJ = 512
TT_LRU = 512
TN_SELECT = 256
TN_DENSE = 512
TE_DENSE = 512
DENSE_SLAB = 16


def _params(*sem):
    return pltpu.CompilerParams(dimension_semantics=sem, vmem_limit_bytes=VMEM_LIMIT_BYTES)


def _split_dot(x, w_bf16):
    hi = x.astype(BF16)
    lo = (x - hi.astype(F32)).astype(BF16)
    return (jnp.dot(hi, w_bf16, preferred_element_type=F32)
            + jnp.dot(lo, w_bf16, preferred_element_type=F32))


def _rms_rows(x):
    return x * lax.rsqrt(jnp.mean(x * x, axis=-1, keepdims=True) + EPS)


def _ada_kernel(c_ref, w_ref, b_ref, o_ref):
    c = c_ref[...]
    o_ref[...] = jnp.dot(jax.nn.silu(c), w_ref[...], precision=HIGHEST,
                         preferred_element_type=F32) + b_ref[...]


def _ada(c_all, w_ada, b_ada):
    depth, _, n_out = w_ada.shape
    rows = c_all.shape[0]
    tn = 512
    return pl.pallas_call(
        _ada_kernel,
        grid=(depth, n_out // tn),
        in_specs=[pl.BlockSpec((rows, D_MODEL), lambda l, j: (0, 0)),
                  pl.BlockSpec((None, D_MODEL, tn), lambda l, j: (l, 0, j)),
                  pl.BlockSpec((None, 1, tn), lambda l, j: (l, 0, j))],
        out_specs=pl.BlockSpec((None, rows, tn), lambda l, j: (l, 0, j)),
        out_shape=jax.ShapeDtypeStruct((depth, rows, n_out), F32),
        compiler_params=_params("parallel", "parallel"),
        name="ada",
    )(c_all, w_ada, b_ada.reshape(depth, 1, n_out))


def _inproj_kernel(x_ref, sh_ref, sc_ref, nm_ref, w_ref, qg_ref, kg_ref, gn_ref, ones_ref,
                   gw_ref, gb_ref, q_ref, k_ref, v_ref, yb_ref, gv_ref, xr_ref, xg_ref, *, chunked):
    x = x_ref[...]
    h = _rms_rows(x) * nm_ref[...] * (1.0 + sc_ref[...]) + sh_ref[...]
    z = jnp.dot(h.astype(BF16), w_ref[...], preferred_element_type=F32)
    ones = ones_ref[...]

    def headnorm(a, g):
        ss = _split_dot(a * a, ones)
        return a * lax.rsqrt(ss * (1.0 / HEAD_DIM) + EPS) * g

    q_ref[...] = headnorm(z[:, 0:D_A], qg_ref[...]) * ATTN_SCALE
    k_ref[...] = headnorm(z[:, D_A:2 * D_A], kg_ref[...])
    v_ref[...] = z[:, 2 * D_A:3 * D_A]
    o = 3 * D_A
    ub = z[:, o:o + D_B]
    vbn = _rms_rows(z[:, o + D_B:o + 2 * D_B]) * gn_ref[...]
    gv_ref[...] = vbn
    xr_ref[...] = z[:, o + 2 * D_B:o + 2 * D_B + D_C]
    xg_ref[...] = z[:, o + 2 * D_B + D_C:o + 2 * D_B + 2 * D_C]
    if chunked:
        rg = lax.broadcasted_iota(jnp.int32, (G_B * CHUNK, D_B), 0) // CHUNK
        cg = lax.broadcasted_iota(jnp.int32, (G_B * CHUNK, D_B), 1) // E_B
        keep = rg == cg
        for ci in range(x.shape[0] // CHUNK):
            rows = slice(ci * CHUNK, (ci + 1) * CHUNK)
            vc = vbn[rows]
            vbd = jnp.where(keep, jnp.concatenate([vc] * G_B, axis=0), 0.0)
            mix = _split_dot_rhs(gw_ref[0], gw_ref[1], vbd) + gb_ref[...]
            yb_ref[rows, :] = ub[rows] * mix
    else:
        yb_ref[...] = ub * (gw_ref[...] * vbn + gb_ref[...])


def _split_dot_rhs(w_hi, w_lo, x):
    x_hi = x.astype(BF16)
    x_lo = (x - x_hi.astype(F32)).astype(BF16)
    return (jnp.dot(w_hi, x_hi, preferred_element_type=F32)
            + jnp.dot(w_hi, x_lo, preferred_element_type=F32)
            + jnp.dot(w_lo, x_hi, preferred_element_type=F32))


def _bf16_pair(w):
    hi = w.astype(BF16)
    lo = (w - hi.astype(F32)).astype(BF16)
    return jnp.stack([hi, lo])


def _inproj(x, sh, sc, lp, *, chunked):
    g, t, _ = x.shape
    tm = min(TM_PROJ, t)
    r = sh.shape[1]
    mod_spec = pl.BlockSpec((None, r, D_MODEL), (lambda b, i: (b, 0, 0)))
    full = lambda a: pl.BlockSpec(a.shape, lambda b, i: (0,) * a.ndim)
    if chunked:
        gw = _bf16_pair(jnp.transpose(jnp.tril(lp['w_s']), (1, 0, 2)).reshape(CHUNK, G_B * CHUNK))
        gb = jnp.repeat(lp['b_s'].T, E_B, axis=1)
    else:
        gw = jnp.repeat(lp['w_s'][:, 0, 0], E_B)[None, :]
        gb = jnp.repeat(lp['b_s'][:, 0], E_B)[None, :]
    weights = [lp['norm_mix'][None, :], lp['w_in'].astype(BF16),
               jnp.tile(lp['q_gain'], H_A)[None, :], jnp.tile(lp['k_gain'], H_A)[None, :],
               lp['gmlp_norm'][None, :], _head_ones(), gw, gb]
    row = lambda d: pl.BlockSpec((None, tm, d), lambda b, i: (b, i, 0))
    widths = (D_A, D_A, D_A, D_B, D_B, D_C, D_C)
    return pl.pallas_call(
        functools.partial(_inproj_kernel, chunked=chunked),
        grid=(g, t // tm),
        in_specs=[row(D_MODEL), mod_spec, mod_spec] + [full(w) for w in weights],
        out_specs=[row(d) for d in widths],
        out_shape=[jax.ShapeDtypeStruct((g, t, d), F32) for d in widths],
        compiler_params=_params("parallel", "parallel"),
        name="inproj_prompt" if chunked else "inproj_sample",
    )(x, sh, sc, *weights)


def _head_ones():
    hid = np.arange(D_A) // HEAD_DIM
    return jnp.asarray((hid[:, None] == hid[None, :]).astype(np.float32), dtype=BF16)


def _t5_bucket(dist):
    exact = N_BUCKETS // 2
    df = np.maximum(dist, 1).astype(np.float64)
    large = exact + (np.log(df / exact) / math.log(MAX_DIST / exact) * (N_BUCKETS - exact)).astype(np.int64)
    return np.where(dist < exact, dist, np.minimum(large, N_BUCKETS - 1))


def _bias_lookup(rel_bias, dist):
    onehot = (_t5_bucket(dist)[..., None] == np.arange(N_BUCKETS)).astype(np.float32)
    return jnp.einsum('...b,bh->...h', jnp.asarray(onehot), rel_bias.astype(F32), precision=HIGHEST)


def _prompt_bias(rel_bias, window, dil):
    qi = np.arange(BLK)[:, None] + BLK
    ki = np.arange(2 * BLK)[None, :]
    dm = qi - ki
    valid = (dm >= 0) & (dm <= window // dil)
    bias = jnp.transpose(_bias_lookup(rel_bias, np.maximum(dm, 0) * dil), (2, 0, 1))
    return jnp.where(jnp.asarray(valid)[None], bias, NEG)


def _attn_prompt_kernel(q_ref, kp_ref, kc_ref, vp_ref, vc_ref, bias_ref, o_ref, lse_ref):
    first = pl.program_id(2) == 0
    q = q_ref[...].astype(BF16)
    k2 = jnp.concatenate([kp_ref[...], kc_ref[...]], axis=0).astype(BF16)
    v2 = jnp.concatenate([vp_ref[...], vc_ref[...]], axis=0).astype(BF16)
    col = lax.broadcasted_iota(jnp.int32, (BLK, 2 * BLK), 1)
    no_prev = jnp.logical_and(first, col < BLK)
    for h in range(H_A):
        hs = slice(h * HEAD_DIM, (h + 1) * HEAD_DIM)
        lg = lax.dot_general(q[:, hs], k2[:, hs], (((1,), (1,)), ((), ())),
                             preferred_element_type=F32) + bias_ref[h]
        lg = jnp.where(no_prev, NEG, lg)
        m = jnp.max(lg, axis=-1, keepdims=True)
        p = jnp.exp(lg - m)
        s = jnp.sum(p, axis=-1, keepdims=True)
        o = jnp.dot(p.astype(BF16), v2[:, hs], preferred_element_type=F32) / s
        o_ref[:, hs] = o
        lse_ref[:, hs] = jnp.broadcast_to(m + jnp.log(s), (BLK, HEAD_DIM))


def _attn_prompt(q, k, v, bias, dil):
    b, s, _ = q.shape
    length = s // dil
    nb = length // BLK
    view = lambda a: a.reshape(b, length, dil * D_A)
    cur = pl.BlockSpec((None, BLK, D_A), lambda i, p, n: (i, n, p))
    prev = pl.BlockSpec((None, BLK, D_A), lambda i, p, n: (i, jnp.maximum(n - 1, 0), p))
    o, lse = pl.pallas_call(
        _attn_prompt_kernel,
        grid=(b, dil, nb),
        in_specs=[cur, prev, cur, prev, cur,
                  pl.BlockSpec(bias.shape, lambda i, p, n: (0, 0, 0))],
        out_specs=[cur, cur],
        out_shape=[jax.ShapeDtypeStruct((b, length, dil * D_A), F32)] * 2,
        compiler_params=_params("parallel", "parallel", "arbitrary"),
        name=f"attn_prompt_d{dil}",
    )(view(q), view(k), view(k), view(v), view(v), bias)
    return o.reshape(b, s, D_A), lse.reshape(b, s, D_A)


def _attn_sample_kernel(q_ref, kn_ref, qt_ref, vnt_ref, b0_ref, kt_ref, vt_ref, br1_ref, br4_ref, br16_ref, ot_ref):
    qt = qt_ref[0]
    vnt = vnt_ref[0]
    w_cache = kt_ref.shape[-1]
    sub = lax.broadcasted_iota(jnp.int32, (H_A, w_cache), 0)
    logits = jnp.zeros((H_A, w_cache), F32)
    for h in range(H_A):
        row = jnp.sum(kt_ref[0, h] * qt[:, h:h + 1], axis=0, keepdims=True)
        logits = jnp.where(sub == h, row, logits)
    l0 = jnp.sum(q_ref[0] * kn_ref[0], axis=1, keepdims=True) + b0_ref[...]
    ps, p0s, ss, lses = [], [], [], []
    for br_ref in (br1_ref, br4_ref, br16_ref):
        span = br_ref.shape[-1]
        lg = logits[:, w_cache - span:] + br_ref[...]
        m = jnp.maximum(jnp.max(lg, axis=1, keepdims=True), l0)
        p = jnp.exp(lg - m)
        p0 = jnp.exp(l0 - m)
        s = jnp.sum(p, axis=1, keepdims=True) + p0
        ps.append(p), p0s.append(p0), ss.append(s), lses.append(m + jnp.log(s))
    mm = jnp.maximum(jnp.maximum(lses[0], lses[1]), lses[2])
    ws = [jnp.exp(l - mm) for l in lses]
    tot = ws[0] + ws[1] + ws[2]
    cs = [w / (s * tot) for w, s in zip(ws, ss)]
    p1, p4, p16 = (c * p for c, p in zip(cs, ps))
    s1, s4 = p1.shape[1], p4.shape[1]
    probs = jnp.concatenate([p16[:, :w_cache - s4],
                             p16[:, w_cache - s4:w_cache - s1] + p4[:, :s4 - s1],
                             p16[:, w_cache - s1:] + p4[:, s4 - s1:] + p1], axis=1)
    prob0 = cs[0] * p0s[0] + cs[1] * p0s[1] + cs[2] * p0s[2]
    for h in range(H_A):
        pv = jnp.sum(vt_ref[0, h] * probs[h:h + 1, :], axis=1, keepdims=True)
        ot_ref[0, :, h:h + 1] = pv + prob0[h:h + 1, :] * vnt[:, h:h + 1]


def _attn_sample(q, kn, vn, cache_k, cache_v, layer, rel_bias):
    nseq = q.shape[0]
    w_cache = cache_k.shape[2]
    heads = lambda a: a.reshape(nseq, H_A, HEAD_DIM)
    cols = lambda a: jnp.transpose(heads(a), (0, 2, 1))
    spans = [window for window, _ in DILATED_PAIRS]
    assert spans == sorted(spans) and w_cache == spans[-1]
    brs = []
    for window, dil in DILATED_PAIRS:
        assert window // dil == BLK
        dist = window - np.arange(window)
        brs.append(jnp.where(jnp.asarray(dist % dil == 0)[None], _bias_lookup(rel_bias, dist).T, NEG))
    b0 = rel_bias[0].astype(F32)[:, None]
    row = pl.BlockSpec((1, H_A, HEAD_DIM), lambda i: (i, 0, 0))
    col = pl.BlockSpec((1, HEAD_DIM, H_A), lambda i: (i, 0, 0))
    cache = pl.BlockSpec((None, 1, H_A, HEAD_DIM, w_cache), lambda i: (layer, i, 0, 0, 0))
    full = lambda a: pl.BlockSpec(a.shape, lambda i: (0,) * a.ndim)
    out = pl.pallas_call(
        _attn_sample_kernel,
        grid=(nseq,),
        in_specs=[row, row, col, col, full(b0), cache, cache] + [full(a) for a in brs],
        out_specs=col,
        out_shape=jax.ShapeDtypeStruct((nseq, HEAD_DIM, H_A), F32),
        compiler_params=_params("parallel"),
        name="attn_sample",
    )(heads(q), heads(kn), cols(q), cols(vn), b0,
      jnp.transpose(cache_k, (0, 1, 3, 4, 2)), jnp.transpose(cache_v, (0, 1, 3, 4, 2)), *brs)
    return jnp.transpose(out, (0, 2, 1)).reshape(nseq, D_A)


def _lru_gates(xc, wa_ref, ba_ref, wx_ref, bx_ref, lam_ref):
    r = jax.nn.sigmoid(jnp.dot(xc, wa_ref[...], precision=HIGHEST, preferred_element_type=F32) + ba_ref[...])
    ig = jax.nn.sigmoid(jnp.dot(xc, wx_ref[...], precision=HIGHEST, preferred_element_type=F32) + bx_ref[...])
    nl = -lam_ref[...]
    softplus = jnp.maximum(nl, 0.0) + jnp.log(1.0 + jnp.exp(-jnp.abs(nl)))
    log_a = -LRU_C * r * softplus
    a = jnp.exp(log_a)
    bterm = jnp.sqrt(1.0 - a * a) * ig * xc
    return a, bterm


def _lru_prompt_kernel(xr_ref, xg_ref, cb_ref, h0_ref, cw_ref, cbias_ref, wa_ref, ba_ref, wx_ref, bx_ref,
                       lam_ref, y_ref, nb_ref, hl_ref, xbuf, a_s, b_s, h_s, hcar):
    tt = xr_ref.shape[0]
    pad = 8

    @pl.when(pl.program_id(1) == 0)
    def _():
        xbuf[0:pad, :] = jnp.zeros((pad, D_C), F32)
        xbuf[pad - (CONV_W - 1):pad, :] = cb_ref[...]
        hcar[...] = h0_ref[...]

    xbuf[pad:pad + tt, :] = xr_ref[...]
    xc = cbias_ref[...]
    for kk in range(CONV_W):
        off = pad - (CONV_W - 1) + kk
        xc = xc + xbuf[off:off + tt, :] * cw_ref[kk:kk + 1, :]
    a, bterm = _lru_gates(xc, wa_ref, ba_ref, wx_ref, bx_ref, lam_ref)
    a_s[...] = a
    b_s[...] = bterm

    def step(i, h):
        h = a_s[pl.ds(i, 1), :] * h + b_s[pl.ds(i, 1), :]
        h_s[pl.ds(i, 1), :] = h
        return h

    h = lax.fori_loop(0, tt, step, hcar[...], unroll=8)
    hcar[...] = h
    hl_ref[...] = h
    y_ref[...] = jax.nn.gelu(xg_ref[...]) * h_s[...]
    tail = xbuf[pad + tt - (CONV_W - 1):pad + tt, :]
    nb_ref[...] = tail
    xbuf[pad - (CONV_W - 1):pad, :] = tail


def _lru_weights(lp):
    def blockdiag(w):
        eye = jnp.eye(H_C, dtype=w.dtype)
        return jnp.einsum('hij,hg->higj', w, eye).reshape(D_C, D_C)
    return [lp['conv_w'], lp['conv_b'][None, :], blockdiag(lp['w_a']), lp['b_a'][None, :],
            blockdiag(lp['w_x']), lp['b_x'][None, :], lp['lru_lambda'][None, :]]


def _lru_prompt(xr, xg, conv_buf, h0, lp):
    b, t, _ = xr.shape
    tt = min(TT_LRU, t)
    assert t % tt == 0 and tt >= CONV_W - 1
    weights = _lru_weights(lp)
    row = pl.BlockSpec((None, tt, D_C), lambda i, j: (i, j, 0))
    per_seq = lambda n: pl.BlockSpec((None, n, D_C), lambda i, j: (i, 0, 0))
    full = lambda a: pl.BlockSpec(a.shape, lambda i, j: (0,) * a.ndim)
    y, nb, hl = pl.pallas_call(
        _lru_prompt_kernel,
        grid=(b, t // tt),
        in_specs=[row, row, per_seq(CONV_W - 1), per_seq(1)] + [full(w) for w in weights],
        out_specs=[row, per_seq(CONV_W - 1), per_seq(1)],
        out_shape=[jax.ShapeDtypeStruct((b, t, D_C), F32),
                   jax.ShapeDtypeStruct((b, CONV_W - 1, D_C), F32),
                   jax.ShapeDtypeStruct((b, 1, D_C), F32)],
        scratch_shapes=[pltpu.VMEM((tt + 8, D_C), F32), pltpu.VMEM((tt, D_C), F32),
                        pltpu.VMEM((tt, D_C), F32), pltpu.VMEM((tt, D_C), F32),
                        pltpu.VMEM((1, D_C), F32)],
        compiler_params=_params("parallel", "arbitrary"),
        name="lru_prompt",
    )(xr, xg, conv_buf, h0[:, None, :], *weights)
    return y, nb, hl[:, 0, :]


def _lru_sample_kernel(xr_ref, xg_ref, c0_ref, c1_ref, c2_ref, h0_ref, cw_ref, cbias_ref, wa_ref, ba_ref,
                       wx_ref, bx_ref, lam_ref, y_ref, hl_ref):
    xc = (cbias_ref[...] + c0_ref[...] * cw_ref[0:1, :] + c1_ref[...] * cw_ref[1:2, :]
          + c2_ref[...] * cw_ref[2:3, :] + xr_ref[...] * cw_ref[3:4, :])
    a, bterm = _lru_gates(xc, wa_ref, ba_ref, wx_ref, bx_ref, lam_ref)
    h = a * h0_ref[...] + bterm
    hl_ref[...] = h
    y_ref[...] = jax.nn.gelu(xg_ref[...]) * h


def _lru_sample(xr, xg, conv_buf, h0, lp):
    n = xr.shape[0]
    weights = _lru_weights(lp)
    args = [xr, xg, conv_buf[:, 0], conv_buf[:, 1], conv_buf[:, 2], h0] + weights
    full = lambda a: pl.BlockSpec(a.shape, lambda i: (0,) * a.ndim)
    return pl.pallas_call(
        _lru_sample_kernel,
        grid=(1,),
        in_specs=[full(a) for a in args],
        out_specs=[full(xr), full(xr)],
        out_shape=[jax.ShapeDtypeStruct((n, D_C), F32)] * 2,
        compiler_params=_params("arbitrary"),
        name="lru_sample",
    )(*args)


def _outproj_kernel(*refs, merge):
    if merge:
        o1, o2, o3, l1, l2, l3 = refs[:6]
        refs = refs[6:]
        lses = [l1[...], l2[...], l3[...]]
        mm = jnp.maximum(jnp.maximum(lses[0], lses[1]), lses[2])
        ws = [jnp.exp(l - mm) for l in lses]
        ya = (ws[0] * o1[...] + ws[1] * o2[...] + ws[2] * o3[...]) / (ws[0] + ws[1] + ws[2])
    else:
        ya = refs[0][...]
        refs = refs[1:]
    yb_ref, yc_ref, x_ref, g1_ref, sh_ref, sc_ref, og_ref, w_ref, nf_ref, xm_ref, h2_ref = refs
    og = og_ref[...]
    mix = jnp.concatenate([_rms_rows(ya) * og[:, 0:D_A],
                           _rms_rows(yb_ref[...]) * og[:, D_A:D_A + D_B],
                           _rms_rows(yc_ref[...]) * og[:, D_A + D_B:]], axis=-1)
    x = x_ref[...] + g1_ref[...] * jnp.dot(mix.astype(BF16), w_ref[...], preferred_element_type=F32)
    xm_ref[...] = x
    h2 = _rms_rows(x) * nf_ref[...] * (1.0 + sc_ref[...]) + sh_ref[...]
    h2_ref[...] = h2.astype(BF16)


def _outproj(attn, yb, yc, x, g1, sh2, sc2, lp):
    g, t, _ = x.shape
    tm = min(TM_PROJ, t)
    merge = len(attn) > 1
    r = g1.shape[1]
    row = lambda d: pl.BlockSpec((None, tm, d), lambda b, i: (b, i, 0))
    mod_spec = pl.BlockSpec((None, r, D_MODEL), lambda b, i: (b, 0, 0))
    full = lambda a: pl.BlockSpec(a.shape, lambda b, i: (0,) * a.ndim)
    weights = [lp['out_gain'][None, :], lp['w_out'].astype(BF16), lp['norm_ffn'][None, :]]
    return pl.pallas_call(
        functools.partial(_outproj_kernel, merge=merge),
        grid=(g, t // tm),
        in_specs=[row(D_A)] * len(attn) + [row(D_B), row(D_C), row(D_MODEL), mod_spec, mod_spec, mod_spec]
                 + [full(w) for w in weights],
        out_specs=[row(D_MODEL), row(D_MODEL)],
        out_shape=[jax.ShapeDtypeStruct((g, t, D_MODEL), F32), jax.ShapeDtypeStruct((g, t, D_MODEL), BF16)],
        compiler_params=_params("parallel", "parallel"),
        name="outproj_prompt" if merge else "outproj_sample",
    )(*attn, yb, yc, x, g1, sh2, sc2, *weights)


def _sort16_pairs():
    pairs = []

    def merge(lo, hi, r):
        step = r * 2
        if step < hi - lo:
            merge(lo, hi, step)
            merge(lo + r, hi, step)
            pairs.extend((i, i + r) for i in range(lo + r, hi - r, step))
        else:
            pairs.append((lo, lo + r))

    def sort(lo, hi):
        if hi - lo >= 1:
            mid = lo + (hi - lo) // 2
            sort(lo, mid)
            sort(mid + 1, hi)
            merge(lo, hi, 1)

    sort(0, PEER_TOPK - 1)
    return pairs


_SORT16 = _sort16_pairs()
_BITONIC16 = [(i, i + d) for d in (8, 4, 2, 1) for i in range(PEER_TOPK) if not i & d]


def _exchange(vals, pairs):
    vals = list(vals)
    for i, j in pairs:
        hi, lo = jnp.maximum(vals[i], vals[j]), jnp.minimum(vals[i], vals[j])
        vals[i], vals[j] = hi, lo
    return vals


def _merge_top16(a, b):
    return _exchange([jnp.maximum(a[i], b[PEER_TOPK - 1 - i]) for i in range(PEER_TOPK)], _BITONIC16)


def _across_sublanes(vals):
    for shift in (4, 2, 1):
        vals = _merge_top16(vals, [pltpu.roll(v, shift, axis=0) for v in vals])
    return vals


def _top16_of_keys(s):
    slabs = [s[8 * i:8 * i + 8, :] for i in range(N_KEYS // 8)]
    return _across_sublanes(_exchange(slabs, _SORT16))


def _prefix_count(holds, vals):
    pick = jnp.where
    m8 = holds(vals[7])
    m4 = holds(pick(m8, vals[11], vals[3]))
    m2 = holds(pick(m8, pick(m4, vals[13], vals[9]), pick(m4, vals[5], vals[1])))
    m1 = holds(pick(m8, pick(m4, pick(m2, vals[14], vals[12]), pick(m2, vals[10], vals[8])),
                    pick(m4, pick(m2, vals[6], vals[4]), pick(m2, vals[2], vals[0]))))
    m0 = holds(vals[15])
    return (pick(m8, 8.0, 0.0) + pick(m4, 4.0, 0.0) + pick(m2, 2.0, 0.0) + pick(m1, 1.0, 0.0)
            + pick(m0, 1.0, 0.0))


def _peer_select_kernel(xt_ref, wq_ref, keys_ref, cnt_ref, e1_ref, rk_ref, e2_ref, q_s):
    tn = xt_ref.shape[1]
    sub = lax.broadcasted_iota(jnp.int32, (8, tn), 0)
    q_s[...] = jnp.dot(wq_ref[...], xt_ref[...], preferred_element_type=F32)
    half = D_QUERY // 2

    def head(h, carry):
        s = []
        for p in range(2):
            qt = q_s[pl.ds(pl.multiple_of((2 * h + p) * half, half), half), :]
            qn = qt * lax.rsqrt(jnp.mean(qt * qt, axis=0, keepdims=True) + EPS)
            s.append(_split_dot_rhs(keys_ref[0, 2 * h + p], keys_ref[1, 2 * h + p], qn))
        s1, s2 = s
        v1 = _top16_of_keys(s1)
        v2 = _top16_of_keys(s2)
        a_lo, a_hi = v1[0], v1[8]
        for j in range(1, 8):
            a_lo = jnp.where(sub == j, v1[j], a_lo)
            a_hi = jnp.where(sub == j, v1[8 + j], a_hi)
        sv = _across_sublanes(_merge_top16([a_lo + b for b in v2], [a_hi + b for b in v2]))
        tau = sv[PEER_TOPK - 1][None]
        z = jnp.ones_like(sv[0])
        for kk in range(1, PEER_TOPK):
            z = z + jnp.exp(sv[kk] - sv[0])
        inv_z = 1.0 / z
        for i in range(N_KEYS // 16):
            rows = slice(16 * i, 16 * i + 16)
            s1r = s1[rows].reshape(2, 8, tn)
            s2r = s2[rows].reshape(2, 8, tn)
            cnt = _prefix_count(lambda v: s1r + v >= tau, v2)
            rk = _prefix_count(lambda v: v > s2r, v2)
            cnt_ref[h, rows, :] = cnt.reshape(16, tn)
            e1_ref[h, rows, :] = jnp.exp(s1r - v1[0][None]).reshape(16, tn)
            rk_ref[h, rows, :] = rk.reshape(16, tn).astype(BF16)
            e2_ref[h, rows, :] = (jnp.exp(s2r - v2[0][None]) * inv_z[None]).reshape(16, tn).astype(BF16)
        return carry

    lax.fori_loop(0, PEER_HEADS, head, 0)


def _peer_select(h2t, lp, tn):
    n = h2t.shape[1]
    wq = lp['peer_wq'].T.astype(BF16)
    keys = _bf16_pair(lp['peer_keys'].reshape(2 * PEER_HEADS, N_KEYS, D_QUERY // 2))
    out = pl.BlockSpec((PEER_HEADS, N_KEYS, tn), lambda i: (0, 0, i))
    shape = lambda dt: jax.ShapeDtypeStruct((PEER_HEADS, N_KEYS, n), dt)
    return pl.pallas_call(
        _peer_select_kernel,
        grid=(n // tn,),
        in_specs=[pl.BlockSpec((D_MODEL, tn), lambda i: (0, i)),
                  pl.BlockSpec(wq.shape, lambda i: (0, 0)),
                  pl.BlockSpec(keys.shape, lambda i: (0, 0, 0, 0))],
        out_specs=[out] * 4,
        out_shape=[shape(F32), shape(F32), shape(BF16), shape(BF16)],
        scratch_shapes=[pltpu.VMEM((PEER_HEADS * D_QUERY, tn), F32)],
        compiler_params=_params("parallel"),
        name="peer_select",
    )(h2t, wq, keys)


def _peer_dense_kernel(xt_ref, u0_ref, u_ref, vt_ref, cnt_ref, e1_ref, rk_ref, e2_ref, xm_ref, g2_ref, o_ref,
                       acc, at_s, m_s, cntb, e1b):
    j = pl.program_id(1)
    last = pl.num_programs(1) - 1
    te, tn = at_s.shape
    n_r = te // N_KEYS
    slab = DENSE_SLAB

    @pl.when(j == 0)
    def _():
        acc[...] = jnp.zeros_like(acc)
        m_s[...] = jnp.zeros_like(m_s)
        at_s[...] = jnp.dot(u0_ref[...], xt_ref[...], preferred_element_type=F32)

    acc[...] += jnp.dot(vt_ref[...], m_s[...], preferred_element_type=F32)

    r0 = jnp.minimum(j, last - 1) * n_r
    for rr in range(n_r):
        for h in range(PEER_HEADS):
            cntb[rr * PEER_HEADS + h] = jnp.broadcast_to(cnt_ref[h, pl.ds(r0 + rr, 1), :], (slab, tn)).astype(BF16)
            e1b[rr * PEER_HEADS + h] = jnp.broadcast_to(e1_ref[h, pl.ds(r0 + rr, 1), :], (slab, tn)).astype(BF16)
    for s in range(te // slab):
        rr = s * slab // N_KEYS
        keys2 = slice(s * slab % N_KEYS, s * slab % N_KEYS + slab)
        experts = slice(s * slab, (s + 1) * slab)
        gate = jnp.zeros((slab, tn), BF16)
        for h in range(PEER_HEADS):
            picked = rk_ref[h, keys2, :] < cntb[rr * PEER_HEADS + h]
            gate = gate + jnp.where(picked, e2_ref[h, keys2, :], 0.0) * e1b[rr * PEER_HEADS + h]
        m_s[experts, :] = gate * jax.nn.gelu(at_s[experts, :]).astype(BF16)

    at_s[...] = jnp.dot(u_ref[...], xt_ref[...], preferred_element_type=F32)

    @pl.when(j == last)
    def _():
        o_ref[...] = xm_ref[...] + g2_ref[...] * acc[...].T


def _peer_dense(h2t, sel, xmid, g2, lp, tn):
    g, t, _ = xmid.shape
    n = g * t
    te = TE_DENSE
    n_exp = lp['peer_u'].shape[0]
    nj = n_exp // te
    u = lp['peer_u'].astype(BF16)
    vt = lp['peer_v'].T.astype(BF16)
    tiles_per_seq = t // tn
    r = g2.shape[1]
    tok = pl.BlockSpec((PEER_HEADS, N_KEYS, tn), lambda i, j: (0, 0, i))
    row = pl.BlockSpec((None, tn, D_MODEL), lambda i, j: (i // tiles_per_seq, i % tiles_per_seq, 0))
    return pl.pallas_call(
        _peer_dense_kernel,
        grid=(n // tn, nj + 1),
        in_specs=[pl.BlockSpec((D_MODEL, tn), lambda i, j: (0, i)),
                  pl.BlockSpec((te, D_MODEL), lambda i, j: (0, 0)),
                  pl.BlockSpec((te, D_MODEL), lambda i, j: (jnp.minimum(j + 1, nj - 1), 0)),
                  pl.BlockSpec((D_MODEL, te), lambda i, j: (0, jnp.maximum(j - 1, 0))),
                  tok, tok, tok, tok, row,
                  pl.BlockSpec((None, r, D_MODEL), lambda i, j: (i // tiles_per_seq, 0, 0))],
        out_specs=row,
        out_shape=jax.ShapeDtypeStruct((g, t, D_MODEL), F32),
        scratch_shapes=[pltpu.VMEM((D_MODEL, tn), F32), pltpu.VMEM((te, tn), F32), pltpu.VMEM((te, tn), BF16),
                        pltpu.VMEM((te // N_KEYS * PEER_HEADS, DENSE_SLAB, tn), BF16),
                        pltpu.VMEM((te // N_KEYS * PEER_HEADS, DENSE_SLAB, tn), BF16)],
        compiler_params=_params("parallel", "arbitrary"),
        name="peer_dense",
    )(h2t, u, u, vt, *sel, xmid, g2)


def _peer(xmid, h2, g2, lp, tn_select, tn_dense):
    g, t, _ = xmid.shape
    h2t = h2.reshape(g * t, D_MODEL).T
    sel = _peer_select(h2t, lp, tn_select)
    return _peer_dense(h2t, sel, xmid, g2, lp, tn_dense)


def _layer_prompt(x, mod, lp, biases):
    sh1, sc1, g1, sh2, sc2, g2 = mod
    b, t, _ = x.shape
    q, k, v, yb, gv, xr, xg = _inproj(x, sh1, sc1, lp, chunked=True)
    branches = [_attn_prompt(q, k, v, bias, dil) for bias, (_, dil) in zip(biases, DILATED_PAIRS)]
    attn = [o for o, _ in branches] + [l for _, l in branches]
    yc, new_buf, h_last = _lru_prompt(xr, xg, jnp.zeros((b, CONV_W - 1, D_C), F32), jnp.zeros((b, D_C), F32), lp)
    xmid, h2 = _outproj(attn, yb, yc, x, g1, sh2, sc2, lp)
    x = _peer(xmid, h2, g2, lp, TN_SELECT, TN_DENSE)
    keep = min(W_MAX, t)
    state = (k[:, t - keep:].reshape(b, keep, H_A, HEAD_DIM), v[:, t - keep:].reshape(b, keep, H_A, HEAD_DIM),
             gv[:, t - CHUNK:], new_buf, h_last)
    return x, state


def _layer_sample(x, mod, lp, rel_bias, cache_k, cache_v, conv_buf, h0, layer):
    sh1, sc1, g1, sh2, sc2, g2 = mod
    _, n, _ = x.shape
    q, k, v, yb, gv, xr, xg = _inproj(x, sh1, sc1, lp, chunked=False)
    ya = _attn_sample(q[0], k[0], v[0], cache_k, cache_v, layer, rel_bias)
    yc, h_last = _lru_sample(xr[0], xg[0], conv_buf, h0, lp)
    xmid, h2 = _outproj([ya[None]], yb, yc[None], x, g1, sh2, sc2, lp)
    x = _peer(xmid, h2, g2, lp, n, n)
    new_buf = jnp.concatenate([conv_buf[:, 1:], xr[0][:, None, :]], axis=1)
    state = (k[0].reshape(n, 1, H_A, HEAD_DIM), v[0].reshape(n, 1, H_A, HEAD_DIM), gv[0][:, None, :],
             new_buf, h_last)
    return x, state


def kernel(x_prompt, x_sample, cache_k, cache_v, state_conv, state_h, c_prompt, c_sample, rel_bias, w_ada, b_ada, norm_mix, norm_ffn, w_in, q_gain, k_gain, gmlp_norm, w_s, b_s, conv_w, conv_b, w_a, b_a, w_x, b_x, lru_lambda, out_gain, w_out, peer_wq, peer_keys, peer_u, peer_v):
    depth = w_in.shape[0]
    nb, ns = x_prompt.shape[0], x_sample.shape[0]
    assert x_sample.shape[1] == 1
    pad = -(nb + ns) % 8
    c_all = jnp.concatenate([c_prompt, c_sample, jnp.zeros((pad, D_MODEL), F32)], axis=0)
    mods = _ada(c_all, w_ada, b_ada)
    biases = [_prompt_bias(rel_bias, window, dil) for window, dil in DILATED_PAIRS]
    xp = x_prompt
    xs = x_sample.reshape(1, ns, D_MODEL)
    st_p, st_s = [], []
    for l in range(depth):
        lp = dict(norm_mix=norm_mix[l], norm_ffn=norm_ffn[l], w_in=w_in[l], q_gain=q_gain[l], k_gain=k_gain[l],
                  gmlp_norm=gmlp_norm[l], w_s=w_s[l], b_s=b_s[l], conv_w=conv_w[l], conv_b=conv_b[l],
                  w_a=w_a[l], b_a=b_a[l], w_x=w_x[l], b_x=b_x[l], lru_lambda=lru_lambda[l],
                  out_gain=out_gain[l], w_out=w_out[l], peer_wq=peer_wq[l], peer_keys=peer_keys[l],
                  peer_u=peer_u[l], peer_v=peer_v[l])
        mod_p = [m[:, None, :] for m in jnp.split(mods[l, :nb], 6, axis=-1)]
        mod_s = [m[None] for m in jnp.split(mods[l, nb:nb + ns], 6, axis=-1)]
        xp, sp = _layer_prompt(xp, mod_p, lp, biases)
        xs, ss = _layer_sample(xs, mod_s, lp, rel_bias, cache_k, cache_v, state_conv[l], state_h[l], l)
        st_p.append(sp)
        st_s.append(ss)
    stack = lambda sts, i: jnp.stack([s[i] for s in sts])
    return (xp, xs.reshape(ns, 1, D_MODEL),
            stack(st_p, 0), stack(st_p, 1), stack(st_s, 0), stack(st_s, 1),
            stack(st_p, 2), stack(st_s, 2), stack(st_p, 3), stack(st_s, 3),
            stack(st_p, 4), stack(st_s, 4))
```

```python
import functools
import math

import jax
import jax.numpy as jnp
import numpy as np
from jax import lax
from jax.experimental import pallas as pl
from jax.experimental.pallas import tpu as pltpu

F32 = jnp.float32
BF16 = jnp.bfloat16
HIGHEST = lax.Precision.HIGHEST

D_MODEL = 1024
HEAD_DIM = 64
D_A = 512
H_A = 8
DILATED_PAIRS = ((128, 1), (512, 4), (2048, 16))
W_MAX = 2048
BLK = 128
ATTN_SCALE = HEAD_DIM ** -0.5
N_BUCKETS = 32
MAX_DIST = 2048
D_B = 256
G_B = 4
E_B = 64
CHUNK = 128
D_C = 256
H_C = 4
E_C = 64
CONV_W = 4
LRU_C = 8.0
PEER_HEADS = 8
N_KEYS = 128
PEER_TOPK = 16
D_QUERY = 256
D_IN = 3 * D_A + 2 * D_B + 2 * D_C
EPS = 1e-6
NEG = -1e30

VMEM_LIMIT_BYTES = 56 * 1024 * 1024

TM_PROJ = 512
LANES = 128
TT_LRU = 512
TN_SELECT = 256
TN_DENSE = 512
TE_DENSE = 512
DENSE_SLAB = 16


def _params(*sem):
    return pltpu.CompilerParams(dimension_semantics=sem, vmem_limit_bytes=VMEM_LIMIT_BYTES)


def _split_dot(x, w_bf16):
    hi = x.astype(BF16)
    lo = (x - hi.astype(F32)).astype(BF16)
    return (jnp.dot(hi, w_bf16, preferred_element_type=F32)
            + jnp.dot(lo, w_bf16, preferred_element_type=F32))


def _rms_rows(x):
    return x * lax.rsqrt(jnp.mean(x * x, axis=-1, keepdims=True) + EPS)


def _ada_kernel(c_ref, w_ref, b_ref, o_ref):
    c = c_ref[...]
    o_ref[...] = jnp.dot(jax.nn.silu(c), w_ref[...], precision=HIGHEST,
                         preferred_element_type=F32) + b_ref[...]


def _ada(c_all, w_ada, b_ada):
    depth, _, n_out = w_ada.shape
    rows = c_all.shape[0]
    tn = 512
    return pl.pallas_call(
        _ada_kernel,
        grid=(depth, n_out // tn),
        in_specs=[pl.BlockSpec((rows, D_MODEL), lambda l, j: (0, 0)),
                  pl.BlockSpec((None, D_MODEL, tn), lambda l, j: (l, 0, j)),
                  pl.BlockSpec((None, 1, tn), lambda l, j: (l, 0, j))],
        out_specs=pl.BlockSpec((None, rows, tn), lambda l, j: (l, 0, j)),
        out_shape=jax.ShapeDtypeStruct((depth, rows, n_out), F32),
        compiler_params=_params("parallel", "parallel"),
        name="ada",
    )(c_all, w_ada, b_ada.reshape(depth, 1, n_out))


def _inproj_kernel(x_ref, sh_ref, sc_ref, nm_ref, w_ref, qg_ref, kg_ref, gn_ref, ones_ref,
                   gw_ref, gb_ref, q_ref, k_ref, v_ref, yb_ref, gv_ref, xr_ref, xg_ref, *, chunked):
    x = x_ref[...]
    h = _rms_rows(x) * nm_ref[...] * (1.0 + sc_ref[...]) + sh_ref[...]
    z = jnp.dot(h.astype(BF16), w_ref[...], preferred_element_type=F32)
    ones = ones_ref[...]

    def headnorm(a, g):
        ss = _split_dot(a * a, ones)
        return a * lax.rsqrt(ss * (1.0 / HEAD_DIM) + EPS) * g

    q_ref[...] = headnorm(z[:, 0:D_A], qg_ref[...]) * ATTN_SCALE
    k_ref[...] = headnorm(z[:, D_A:2 * D_A], kg_ref[...])
    v_ref[...] = z[:, 2 * D_A:3 * D_A]
    o = 3 * D_A
    ub = z[:, o:o + D_B]
    vbn = _rms_rows(z[:, o + D_B:o + 2 * D_B]) * gn_ref[...]
    gv_ref[...] = vbn
    xr_ref[...] = z[:, o + 2 * D_B:o + 2 * D_B + D_C]
    xg_ref[...] = z[:, o + 2 * D_B + D_C:o + 2 * D_B + 2 * D_C]
    if chunked:
        rg = lax.broadcasted_iota(jnp.int32, (G_B * CHUNK, D_B), 0) // CHUNK
        cg = lax.broadcasted_iota(jnp.int32, (G_B * CHUNK, D_B), 1) // E_B
        keep = rg == cg
        for ci in range(x.shape[0] // CHUNK):
            rows = slice(ci * CHUNK, (ci + 1) * CHUNK)
            vc = vbn[rows]
            vbd = jnp.where(keep, jnp.concatenate([vc] * G_B, axis=0), 0.0)
            mix = _split_dot_rhs(gw_ref[0], gw_ref[1], vbd) + gb_ref[...]
            yb_ref[rows, :] = ub[rows] * mix
    else:
        yb_ref[...] = ub * (gw_ref[...] * vbn + gb_ref[...])


def _split_dot_rhs(w_hi, w_lo, x):
    x_hi = x.astype(BF16)
    x_lo = (x - x_hi.astype(F32)).astype(BF16)
    return (jnp.dot(w_hi, x_hi, preferred_element_type=F32)
            + jnp.dot(w_hi, x_lo, preferred_element_type=F32)
            + jnp.dot(w_lo, x_hi, preferred_element_type=F32))


def _bf16_pair(w):
    hi = w.astype(BF16)
    lo = (w - hi.astype(F32)).astype(BF16)
    return jnp.stack([hi, lo])


def _inproj(x, sh, sc, lp, *, chunked):
    g, t, _ = x.shape
    tm = min(TM_PROJ, t)
    r = sh.shape[1]
    mod_spec = pl.BlockSpec((None, r, D_MODEL), (lambda b, i: (b, 0, 0)))
    full = lambda a: pl.BlockSpec(a.shape, lambda b, i: (0,) * a.ndim)
    if chunked:
        gw = _bf16_pair(jnp.transpose(jnp.tril(lp['w_s']), (1, 0, 2)).reshape(CHUNK, G_B * CHUNK))
        gb = jnp.repeat(lp['b_s'].T, E_B, axis=1)
    else:
        gw = jnp.repeat(lp['w_s'][:, 0, 0], E_B)[None, :]
        gb = jnp.repeat(lp['b_s'][:, 0], E_B)[None, :]
    weights = [lp['norm_mix'][None, :], lp['w_in'].astype(BF16),
               jnp.tile(lp['q_gain'], H_A)[None, :], jnp.tile(lp['k_gain'], H_A)[None, :],
               lp['gmlp_norm'][None, :], _head_ones(), gw, gb]
    row = lambda d: pl.BlockSpec((None, tm, d), lambda b, i: (b, i, 0))
    widths = (D_A, D_A, D_A, D_B, D_B, D_C, D_C)
    return pl.pallas_call(
        functools.partial(_inproj_kernel, chunked=chunked),
        grid=(g, t // tm),
        in_specs=[row(D_MODEL), mod_spec, mod_spec] + [full(w) for w in weights],
        out_specs=[row(d) for d in widths],
        out_shape=[jax.ShapeDtypeStruct((g, t, d), F32) for d in widths],
        compiler_params=_params("parallel", "parallel"),
        name="inproj_prompt" if chunked else "inproj_sample",
    )(x, sh, sc, *weights)


def _head_ones():
    hid = np.arange(D_A) // HEAD_DIM
    return jnp.asarray((hid[:, None] == hid[None, :]).astype(np.float32), dtype=BF16)


def _t5_bucket(dist):
    exact = N_BUCKETS // 2
    df = np.maximum(dist, 1).astype(np.float64)
    large = exact + (np.log(df / exact) / math.log(MAX_DIST / exact) * (N_BUCKETS - exact)).astype(np.int64)
    return np.where(dist < exact, dist, np.minimum(large, N_BUCKETS - 1))


def _bias_lookup(rel_bias, dist):
    onehot = (_t5_bucket(dist)[..., None] == np.arange(N_BUCKETS)).astype(np.float32)
    return jnp.einsum('...b,bh->...h', jnp.asarray(onehot), rel_bias.astype(F32), precision=HIGHEST)


def _prompt_bias(rel_bias, window, dil):
    qi = np.arange(BLK)[:, None] + BLK
    ki = np.arange(2 * BLK)[None, :]
    dm = qi - ki
    valid = (dm >= 0) & (dm <= window // dil)
    bias = jnp.transpose(_bias_lookup(rel_bias, np.maximum(dm, 0) * dil), (2, 0, 1))
    return jnp.where(jnp.asarray(valid)[None], bias, NEG)


def _attn_prompt_kernel(q_ref, kp_ref, kc_ref, vp_ref, vc_ref, bias_ref, o_ref, lse_ref, o_s, l_s, *, dil):
    heads = q_ref.shape[-1] // HEAD_DIM
    h0 = pl.program_id(2) * heads
    col = lax.broadcasted_iota(jnp.int32, (BLK, 2 * BLK), 1)
    no_prev = jnp.logical_and(pl.program_id(1) == 0, col < BLK)

    def phase(p, carry):
        rows = pl.ds(p, BLK, stride=dil) if dil > 1 else slice(None)
        q = q_ref[rows, :].astype(BF16)
        k2 = jnp.concatenate([kp_ref[rows, :], kc_ref[rows, :]], axis=0).astype(BF16)
        v2 = jnp.concatenate([vp_ref[rows, :], vc_ref[rows, :]], axis=0).astype(BF16)
        for h in range(heads):
            hs = slice(h * HEAD_DIM, (h + 1) * HEAD_DIM)
            lg = lax.dot_general(q[:, hs], k2[:, hs], (((1,), (1,)), ((), ())),
                                 preferred_element_type=F32) + bias_ref[h0 + h]
            lg = jnp.where(no_prev, NEG, lg)
            m = jnp.max(lg, axis=-1, keepdims=True)
            pr = jnp.exp(lg - m)
            s = jnp.sum(pr, axis=-1, keepdims=True)
            o_s[:, hs] = jnp.dot(pr.astype(BF16), v2[:, hs], preferred_element_type=F32) / s
            l_s[:, hs] = jnp.broadcast_to(m + jnp.log(s), (BLK, HEAD_DIM))
        o_ref[rows, :] = o_s[...]
        lse_ref[rows, :] = l_s[...]
        return carry

    if dil == 1:
        phase(0, 0)
    else:
        lax.fori_loop(0, dil, phase, 0, unroll=H_A // heads)


def _attn_prompt(q, k, v, bias, dil):
    b, s, _ = q.shape
    span = BLK * dil
    heads = H_A if dil == 1 else LANES // HEAD_DIM
    assert s % span == 0 and H_A % heads == 0
    width = heads * HEAD_DIM
    cur = pl.BlockSpec((None, span, width), lambda i, n, g: (i, n, g))
    prev = pl.BlockSpec((None, span, width), lambda i, n, g: (i, jnp.maximum(n - 1, 0), g))
    return pl.pallas_call(
        functools.partial(_attn_prompt_kernel, dil=dil),
        grid=(b, s // span, H_A // heads),
        in_specs=[cur, prev, cur, prev, cur,
                  pl.BlockSpec(bias.shape, lambda i, n, g: (0, 0, 0))],
        out_specs=[cur, cur],
        out_shape=[jax.ShapeDtypeStruct((b, s, D_A), F32)] * 2,
        scratch_shapes=[pltpu.VMEM((BLK, width), F32), pltpu.VMEM((BLK, width), F32)],
        compiler_params=_params("parallel", "arbitrary", "arbitrary"),
        name=f"attn_prompt_d{dil}",
    )(q, k, k, v, v, bias)


def _attn_sample_kernel(q_ref, kn_ref, qt_ref, vnt_ref, b0_ref, kt_ref, vt_ref, br1_ref, br4_ref, br16_ref, ot_ref):
    qt = qt_ref[0]
    vnt = vnt_ref[0]
    w_cache = kt_ref.shape[-1]
    sub = lax.broadcasted_iota(jnp.int32, (H_A, w_cache), 0)
    logits = jnp.zeros((H_A, w_cache), F32)
    for h in range(H_A):
        row = jnp.sum(kt_ref[0, h] * qt[:, h:h + 1], axis=0, keepdims=True)
        logits = jnp.where(sub == h, row, logits)
    l0 = jnp.sum(q_ref[0] * kn_ref[0], axis=1, keepdims=True) + b0_ref[...]
    ps, p0s, ss, lses = [], [], [], []
    for br_ref in (br1_ref, br4_ref, br16_ref):
        span = br_ref.shape[-1]
        lg = logits[:, w_cache - span:] + br_ref[...]
        m = jnp.maximum(jnp.max(lg, axis=1, keepdims=True), l0)
        p = jnp.exp(lg - m)
        p0 = jnp.exp(l0 - m)
        s = jnp.sum(p, axis=1, keepdims=True) + p0
        ps.append(p), p0s.append(p0), ss.append(s), lses.append(m + jnp.log(s))
    mm = jnp.maximum(jnp.maximum(lses[0], lses[1]), lses[2])
    ws = [jnp.exp(l - mm) for l in lses]
    tot = ws[0] + ws[1] + ws[2]
    cs = [w / (s * tot) for w, s in zip(ws, ss)]
    p1, p4, p16 = (c * p for c, p in zip(cs, ps))
    s1, s4 = p1.shape[1], p4.shape[1]
    probs = jnp.concatenate([p16[:, :w_cache - s4],
                             p16[:, w_cache - s4:w_cache - s1] + p4[:, :s4 - s1],
                             p16[:, w_cache - s1:] + p4[:, s4 - s1:] + p1], axis=1)
    prob0 = cs[0] * p0s[0] + cs[1] * p0s[1] + cs[2] * p0s[2]
    for h in range(H_A):
        pv = jnp.sum(vt_ref[0, h] * probs[h:h + 1, :], axis=1, keepdims=True)
        ot_ref[0, :, h:h + 1] = pv + prob0[h:h + 1, :] * vnt[:, h:h + 1]


def _attn_sample(q, kn, vn, cache_k, cache_v, layer, rel_bias):
    nseq = q.shape[0]
    w_cache = cache_k.shape[2]
    heads = lambda a: a.reshape(nseq, H_A, HEAD_DIM)
    cols = lambda a: jnp.transpose(heads(a), (0, 2, 1))
    spans = [window for window, _ in DILATED_PAIRS]
    assert spans == sorted(spans) and w_cache == spans[-1]
    brs = []
    for window, dil in DILATED_PAIRS:
        assert window // dil == BLK
        dist = window - np.arange(window)
        brs.append(jnp.where(jnp.asarray(dist % dil == 0)[None], _bias_lookup(rel_bias, dist).T, NEG))
    b0 = rel_bias[0].astype(F32)[:, None]
    row = pl.BlockSpec((1, H_A, HEAD_DIM), lambda i: (i, 0, 0))
    col = pl.BlockSpec((1, HEAD_DIM, H_A), lambda i: (i, 0, 0))
    cache = pl.BlockSpec((None, 1, H_A, HEAD_DIM, w_cache), lambda i: (layer, i, 0, 0, 0))
    full = lambda a: pl.BlockSpec(a.shape, lambda i: (0,) * a.ndim)
    out = pl.pallas_call(
        _attn_sample_kernel,
        grid=(nseq,),
        in_specs=[row, row, col, col, full(b0), cache, cache] + [full(a) for a in brs],
        out_specs=col,
        out_shape=jax.ShapeDtypeStruct((nseq, HEAD_DIM, H_A), F32),
        compiler_params=_params("parallel"),
        name="attn_sample",
    )(heads(q), heads(kn), cols(q), cols(vn), b0,
      jnp.transpose(cache_k, (0, 1, 3, 4, 2)), jnp.transpose(cache_v, (0, 1, 3, 4, 2)), *brs)
    return jnp.transpose(out, (0, 2, 1)).reshape(nseq, D_A)


def _lru_gates(xc, wa_ref, ba_ref, wx_ref, bx_ref, lam_ref):
    r = jax.nn.sigmoid(jnp.dot(xc, wa_ref[...], precision=HIGHEST, preferred_element_type=F32) + ba_ref[...])
    ig = jax.nn.sigmoid(jnp.dot(xc, wx_ref[...], precision=HIGHEST, preferred_element_type=F32) + bx_ref[...])
    nl = -lam_ref[...]
    softplus = jnp.maximum(nl, 0.0) + jnp.log(1.0 + jnp.exp(-jnp.abs(nl)))
    log_a = -LRU_C * r * softplus
    a = jnp.exp(log_a)
    bterm = jnp.sqrt(1.0 - a * a) * ig * xc
    return a, bterm


def _lru_prompt_kernel(xr_ref, xg_ref, cb_ref, h0_ref, cw_ref, cbias_ref, wa_ref, ba_ref, wx_ref, bx_ref,
                       lam_ref, y_ref, nb_ref, hl_ref, xbuf, a_s, b_s, h_s, hcar):
    tt = xr_ref.shape[0]
    pad = 8

    @pl.when(pl.program_id(1) == 0)
    def _():
        xbuf[0:pad, :] = jnp.zeros((pad, D_C), F32)
        xbuf[pad - (CONV_W - 1):pad, :] = cb_ref[...]
        hcar[...] = h0_ref[...]

    xbuf[pad:pad + tt, :] = xr_ref[...]
    xc = cbias_ref[...]
    for kk in range(CONV_W):
        off = pad - (CONV_W - 1) + kk
        xc = xc + xbuf[off:off + tt, :] * cw_ref[kk:kk + 1, :]
    a, bterm = _lru_gates(xc, wa_ref, ba_ref, wx_ref, bx_ref, lam_ref)
    a_s[...] = a
    b_s[...] = bterm

    def step(i, h):
        h = a_s[pl.ds(i, 1), :] * h + b_s[pl.ds(i, 1), :]
        h_s[pl.ds(i, 1), :] = h
        return h

    h = lax.fori_loop(0, tt, step, hcar[...], unroll=8)
    hcar[...] = h
    hl_ref[...] = h
    y_ref[...] = jax.nn.gelu(xg_ref[...]) * h_s[...]
    tail = xbuf[pad + tt - (CONV_W - 1):pad + tt, :]
    nb_ref[...] = tail
    xbuf[pad - (CONV_W - 1):pad, :] = tail


def _lru_weights(lp):
    def blockdiag(w):
        eye = jnp.eye(H_C, dtype=w.dtype)
        return jnp.einsum('hij,hg->higj', w, eye).reshape(D_C, D_C)
    return [lp['conv_w'], lp['conv_b'][None, :], blockdiag(lp['w_a']), lp['b_a'][None, :],
            blockdiag(lp['w_x']), lp['b_x'][None, :], lp['lru_lambda'][None, :]]


def _lru_prompt(xr, xg, conv_buf, h0, lp):
    b, t, _ = xr.shape
    tt = min(TT_LRU, t)
    assert t % tt == 0 and tt >= CONV_W - 1
    weights = _lru_weights(lp)
    row = pl.BlockSpec((None, tt, D_C), lambda i, j: (i, j, 0))
    per_seq = lambda n: pl.BlockSpec((None, n, D_C), lambda i, j: (i, 0, 0))
    full = lambda a: pl.BlockSpec(a.shape, lambda i, j: (0,) * a.ndim)
    y, nb, hl = pl.pallas_call(
        _lru_prompt_kernel,
        grid=(b, t // tt),
        in_specs=[row, row, per_seq(CONV_W - 1), per_seq(1)] + [full(w) for w in weights],
        out_specs=[row, per_seq(CONV_W - 1), per_seq(1)],
        out_shape=[jax.ShapeDtypeStruct((b, t, D_C), F32),
                   jax.ShapeDtypeStruct((b, CONV_W - 1, D_C), F32),
                   jax.ShapeDtypeStruct((b, 1, D_C), F32)],
        scratch_shapes=[pltpu.VMEM((tt + 8, D_C), F32), pltpu.VMEM((tt, D_C), F32),
                        pltpu.VMEM((tt, D_C), F32), pltpu.VMEM((tt, D_C), F32),
                        pltpu.VMEM((1, D_C), F32)],
        compiler_params=_params("parallel", "arbitrary"),
        name="lru_prompt",
    )(xr, xg, conv_buf, h0[:, None, :], *weights)
    return y, nb, hl[:, 0, :]


def _lru_sample_kernel(xr_ref, xg_ref, c0_ref, c1_ref, c2_ref, h0_ref, cw_ref, cbias_ref, wa_ref, ba_ref,
                       wx_ref, bx_ref, lam_ref, y_ref, hl_ref):
    xc = (cbias_ref[...] + c0_ref[...] * cw_ref[0:1, :] + c1_ref[...] * cw_ref[1:2, :]
          + c2_ref[...] * cw_ref[2:3, :] + xr_ref[...] * cw_ref[3:4, :])
    a, bterm = _lru_gates(xc, wa_ref, ba_ref, wx_ref, bx_ref, lam_ref)
    h = a * h0_ref[...] + bterm
    hl_ref[...] = h
    y_ref[...] = jax.nn.gelu(xg_ref[...]) * h


def _lru_sample(xr, xg, conv_buf, h0, lp):
    n = xr.shape[0]
    weights = _lru_weights(lp)
    args = [xr, xg, conv_buf[:, 0], conv_buf[:, 1], conv_buf[:, 2], h0] + weights
    full = lambda a: pl.BlockSpec(a.shape, lambda i: (0,) * a.ndim)
    return pl.pallas_call(
        _lru_sample_kernel,
        grid=(1,),
        in_specs=[full(a) for a in args],
        out_specs=[full(xr), full(xr)],
        out_shape=[jax.ShapeDtypeStruct((n, D_C), F32)] * 2,
        compiler_params=_params("arbitrary"),
        name="lru_sample",
    )(*args)


def _outproj_kernel(*refs, merge):
    if merge:
        o1, o2, o3, l1, l2, l3 = refs[:6]
        refs = refs[6:]
        lses = [l1[...], l2[...], l3[...]]
        mm = jnp.maximum(jnp.maximum(lses[0], lses[1]), lses[2])
        ws = [jnp.exp(l - mm) for l in lses]
        ya = (ws[0] * o1[...] + ws[1] * o2[...] + ws[2] * o3[...]) / (ws[0] + ws[1] + ws[2])
    else:
        ya = refs[0][...]
        refs = refs[1:]
    yb_ref, yc_ref, x_ref, g1_ref, sh_ref, sc_ref, og_ref, w_ref, nf_ref, xm_ref, h2_ref = refs
    og = og_ref[...]
    mix = jnp.concatenate([_rms_rows(ya) * og[:, 0:D_A],
                           _rms_rows(yb_ref[...]) * og[:, D_A:D_A + D_B],
                           _rms_rows(yc_ref[...]) * og[:, D_A + D_B:]], axis=-1)
    x = x_ref[...] + g1_ref[...] * jnp.dot(mix.astype(BF16), w_ref[...], preferred_element_type=F32)
    xm_ref[...] = x
    h2 = _rms_rows(x) * nf_ref[...] * (1.0 + sc_ref[...]) + sh_ref[...]
    h2_ref[...] = h2.astype(BF16)


def _outproj(attn, yb, yc, x, g1, sh2, sc2, lp):
    g, t, _ = x.shape
    tm = min(TM_PROJ, t)
    merge = len(attn) > 1
    r = g1.shape[1]
    row = lambda d: pl.BlockSpec((None, tm, d), lambda b, i: (b, i, 0))
    mod_spec = pl.BlockSpec((None, r, D_MODEL), lambda b, i: (b, 0, 0))
    full = lambda a: pl.BlockSpec(a.shape, lambda b, i: (0,) * a.ndim)
    weights = [lp['out_gain'][None, :], lp['w_out'].astype(BF16), lp['norm_ffn'][None, :]]
    return pl.pallas_call(
        functools.partial(_outproj_kernel, merge=merge),
        grid=(g, t // tm),
        in_specs=[row(D_A)] * len(attn) + [row(D_B), row(D_C), row(D_MODEL), mod_spec, mod_spec, mod_spec]
                 + [full(w) for w in weights],
        out_specs=[row(D_MODEL), row(D_MODEL)],
        out_shape=[jax.ShapeDtypeStruct((g, t, D_MODEL), F32), jax.ShapeDtypeStruct((g, t, D_MODEL), BF16)],
        compiler_params=_params("parallel", "parallel"),
        name="outproj_prompt" if merge else "outproj_sample",
    )(*attn, yb, yc, x, g1, sh2, sc2, *weights)


def _sort16_pairs():
    pairs = []

    def merge(lo, hi, r):
        step = r * 2
        if step < hi - lo:
            merge(lo, hi, step)
            merge(lo + r, hi, step)
            pairs.extend((i, i + r) for i in range(lo + r, hi - r, step))
        else:
            pairs.append((lo, lo + r))

    def sort(lo, hi):
        if hi - lo >= 1:
            mid = lo + (hi - lo) // 2
            sort(lo, mid)
            sort(mid + 1, hi)
            merge(lo, hi, 1)

    sort(0, PEER_TOPK - 1)
    return pairs


_SORT16 = _sort16_pairs()
_BITONIC16 = [(i, i + d) for d in (8, 4, 2, 1) for i in range(PEER_TOPK) if not i & d]


def _exchange(vals, pairs):
    vals = list(vals)
    for i, j in pairs:
        hi, lo = jnp.maximum(vals[i], vals[j]), jnp.minimum(vals[i], vals[j])
        vals[i], vals[j] = hi, lo
    return vals


def _merge_top16(a, b):
    return _exchange([jnp.maximum(a[i], b[PEER_TOPK - 1 - i]) for i in range(PEER_TOPK)], _BITONIC16)


def _across_sublanes(vals):
    for shift in (4, 2, 1):
        vals = _merge_top16(vals, [pltpu.roll(v, shift, axis=0) for v in vals])
    return vals


def _top16_of_keys(s):
    slabs = [s[8 * i:8 * i + 8, :] for i in range(N_KEYS // 8)]
    return _across_sublanes(_exchange(slabs, _SORT16))


def _prefix_count(holds, vals):
    pick = jnp.where
    m8 = holds(vals[7])
    m4 = holds(pick(m8, vals[11], vals[3]))
    m2 = holds(pick(m8, pick(m4, vals[13], vals[9]), pick(m4, vals[5], vals[1])))
    m1 = holds(pick(m8, pick(m4, pick(m2, vals[14], vals[12]), pick(m2, vals[10], vals[8])),
                    pick(m4, pick(m2, vals[6], vals[4]), pick(m2, vals[2], vals[0]))))
    m0 = holds(vals[15])
    return (pick(m8, 8.0, 0.0) + pick(m4, 4.0, 0.0) + pick(m2, 2.0, 0.0) + pick(m1, 1.0, 0.0)
            + pick(m0, 1.0, 0.0))


def _peer_select_kernel(xt_ref, wq_ref, keys_ref, cnt_ref, e1_ref, rk_ref, e2_ref, q_s):
    tn = xt_ref.shape[1]
    sub = lax.broadcasted_iota(jnp.int32, (8, tn), 0)
    q_s[...] = jnp.dot(wq_ref[...], xt_ref[...], preferred_element_type=F32)
    half = D_QUERY // 2

    def head(h, carry):
        s = []
        for p in range(2):
            qt = q_s[pl.ds(pl.multiple_of((2 * h + p) * half, half), half), :]
            qn = qt * lax.rsqrt(jnp.mean(qt * qt, axis=0, keepdims=True) + EPS)
            s.append(_split_dot_rhs(keys_ref[0, 2 * h + p], keys_ref[1, 2 * h + p], qn))
        s1, s2 = s
        v1 = _top16_of_keys(s1)
        v2 = _top16_of_keys(s2)
        a_lo, a_hi = v1[0], v1[8]
        for j in range(1, 8):
            a_lo = jnp.where(sub == j, v1[j], a_lo)
            a_hi = jnp.where(sub == j, v1[8 + j], a_hi)
        sv = _across_sublanes(_merge_top16([a_lo + b for b in v2], [a_hi + b for b in v2]))
        tau = sv[PEER_TOPK - 1][None]
        z = jnp.ones_like(sv[0])
        for kk in range(1, PEER_TOPK):
            z = z + jnp.exp(sv[kk] - sv[0])
        inv_z = 1.0 / z
        for i in range(N_KEYS // 16):
            rows = slice(16 * i, 16 * i + 16)
            s1r = s1[rows].reshape(2, 8, tn)
            s2r = s2[rows].reshape(2, 8, tn)
            cnt = _prefix_count(lambda v: s1r + v >= tau, v2)
            rk = _prefix_count(lambda v: v > s2r, v2)
            cnt_ref[h, rows, :] = cnt.reshape(16, tn)
            e1_ref[h, rows, :] = jnp.exp(s1r - v1[0][None]).reshape(16, tn)
            rk_ref[h, rows, :] = rk.reshape(16, tn).astype(BF16)
            e2_ref[h, rows, :] = (jnp.exp(s2r - v2[0][None]) * inv_z[None]).reshape(16, tn).astype(BF16)
        return carry

    lax.fori_loop(0, PEER_HEADS, head, 0)


def _peer_select(h2t, lp, tn):
    n = h2t.shape[1]
    wq = lp['peer_wq'].T.astype(BF16)
    keys = _bf16_pair(lp['peer_keys'].reshape(2 * PEER_HEADS, N_KEYS, D_QUERY // 2))
    out = pl.BlockSpec((PEER_HEADS, N_KEYS, tn), lambda i: (0, 0, i))
    shape = lambda dt: jax.ShapeDtypeStruct((PEER_HEADS, N_KEYS, n), dt)
    return pl.pallas_call(
        _peer_select_kernel,
        grid=(n // tn,),
        in_specs=[pl.BlockSpec((D_MODEL, tn), lambda i: (0, i)),
                  pl.BlockSpec(wq.shape, lambda i: (0, 0)),
                  pl.BlockSpec(keys.shape, lambda i: (0, 0, 0, 0))],
        out_specs=[out] * 4,
        out_shape=[shape(F32), shape(F32), shape(BF16), shape(BF16)],
        scratch_shapes=[pltpu.VMEM((PEER_HEADS * D_QUERY, tn), F32)],
        compiler_params=_params("parallel"),
        name="peer_select",
    )(h2t, wq, keys)


def _peer_dense_kernel(xt_ref, u0_ref, u_ref, vt_ref, cnt_ref, e1_ref, rk_ref, e2_ref, xm_ref, g2_ref, o_ref,
                       acc, at_s, m_s, cntb, e1b):
    j = pl.program_id(1)
    last = pl.num_programs(1) - 1
    te, tn = at_s.shape
    n_r = te // N_KEYS
    slab = DENSE_SLAB

    @pl.when(j == 0)
    def _():
        acc[...] = jnp.zeros_like(acc)
        m_s[...] = jnp.zeros_like(m_s)
        at_s[...] = jnp.dot(u0_ref[...], xt_ref[...], preferred_element_type=F32)

    acc[...] += jnp.dot(vt_ref[...], m_s[...], preferred_element_type=F32)

    r0 = jnp.minimum(j, last - 1) * n_r
    for rr in range(n_r):
        for h in range(PEER_HEADS):
            cntb[rr * PEER_HEADS + h] = jnp.broadcast_to(cnt_ref[h, pl.ds(r0 + rr, 1), :], (slab, tn)).astype(BF16)
            e1b[rr * PEER_HEADS + h] = jnp.broadcast_to(e1_ref[h, pl.ds(r0 + rr, 1), :], (slab, tn)).astype(BF16)
    for s in range(te // slab):
        rr = s * slab // N_KEYS
        keys2 = slice(s * slab % N_KEYS, s * slab % N_KEYS + slab)
        experts = slice(s * slab, (s + 1) * slab)
        gate = jnp.zeros((slab, tn), BF16)
        for h in range(PEER_HEADS):
            picked = rk_ref[h, keys2, :] < cntb[rr * PEER_HEADS + h]
            gate = gate + jnp.where(picked, e2_ref[h, keys2, :], 0.0) * e1b[rr * PEER_HEADS + h]
        m_s[experts, :] = gate * jax.nn.gelu(at_s[experts, :]).astype(BF16)

    at_s[...] = jnp.dot(u_ref[...], xt_ref[...], preferred_element_type=F32)

    @pl.when(j == last)
    def _():
        o_ref[...] = xm_ref[...] + g2_ref[...] * acc[...].T


def _peer_dense(h2t, sel, xmid, g2, lp, tn):
    g, t, _ = xmid.shape
    n = g * t
    te = TE_DENSE
    n_exp = lp['peer_u'].shape[0]
    nj = n_exp // te
    u = lp['peer_u'].astype(BF16)
    vt = lp['peer_v'].T.astype(BF16)
    tiles_per_seq = t // tn
    r = g2.shape[1]
    tok = pl.BlockSpec((PEER_HEADS, N_KEYS, tn), lambda i, j: (0, 0, i))
    row = pl.BlockSpec((None, tn, D_MODEL), lambda i, j: (i // tiles_per_seq, i % tiles_per_seq, 0))
    return pl.pallas_call(
        _peer_dense_kernel,
        grid=(n // tn, nj + 1),
        in_specs=[pl.BlockSpec((D_MODEL, tn), lambda i, j: (0, i)),
                  pl.BlockSpec((te, D_MODEL), lambda i, j: (0, 0)),
                  pl.BlockSpec((te, D_MODEL), lambda i, j: (jnp.minimum(j + 1, nj - 1), 0)),
                  pl.BlockSpec((D_MODEL, te), lambda i, j: (0, jnp.maximum(j - 1, 0))),
                  tok, tok, tok, tok, row,
                  pl.BlockSpec((None, r, D_MODEL), lambda i, j: (i // tiles_per_seq, 0, 0))],
        out_specs=row,
        out_shape=jax.ShapeDtypeStruct((g, t, D_MODEL), F32),
        scratch_shapes=[pltpu.VMEM((D_MODEL, tn), F32), pltpu.VMEM((te, tn), F32), pltpu.VMEM((te, tn), BF16),
                        pltpu.VMEM((te // N_KEYS * PEER_HEADS, DENSE_SLAB, tn), BF16),
                        pltpu.VMEM((te // N_KEYS * PEER_HEADS, DENSE_SLAB, tn), BF16)],
        compiler_params=_params("parallel", "arbitrary"),
        name="peer_dense",
    )(h2t, u, u, vt, *sel, xmid, g2)


def _peer(xmid, h2, g2, lp, tn_select, tn_dense):
    g, t, _ = xmid.shape
    h2t = h2.reshape(g * t, D_MODEL).T
    sel = _peer_select(h2t, lp, tn_select)
    return _peer_dense(h2t, sel, xmid, g2, lp, tn_dense)


def _layer_prompt(x, mod, lp, biases):
    sh1, sc1, g1, sh2, sc2, g2 = mod
    b, t, _ = x.shape
    q, k, v, yb, gv, xr, xg = _inproj(x, sh1, sc1, lp, chunked=True)
    branches = [_attn_prompt(q, k, v, bias, dil) for bias, (_, dil) in zip(biases, DILATED_PAIRS)]
    attn = [o for o, _ in branches] + [l for _, l in branches]
    yc, new_buf, h_last = _lru_prompt(xr, xg, jnp.zeros((b, CONV_W - 1, D_C), F32), jnp.zeros((b, D_C), F32), lp)
    xmid, h2 = _outproj(attn, yb, yc, x, g1, sh2, sc2, lp)
    x = _peer(xmid, h2, g2, lp, TN_SELECT, TN_DENSE)
    keep = min(W_MAX, t)
    state = (k[:, t - keep:].reshape(b, keep, H_A, HEAD_DIM), v[:, t - keep:].reshape(b, keep, H_A, HEAD_DIM),
             gv[:, t - CHUNK:], new_buf, h_last)
    return x, state


def _layer_sample(x, mod, lp, rel_bias, cache_k, cache_v, conv_buf, h0, layer):
    sh1, sc1, g1, sh2, sc2, g2 = mod
    _, n, _ = x.shape
    q, k, v, yb, gv, xr, xg = _inproj(x, sh1, sc1, lp, chunked=False)
    ya = _attn_sample(q[0], k[0], v[0], cache_k, cache_v, layer, rel_bias)
    yc, h_last = _lru_sample(xr[0], xg[0], conv_buf, h0, lp)
    xmid, h2 = _outproj([ya[None]], yb, yc[None], x, g1, sh2, sc2, lp)
    x = _peer(xmid, h2, g2, lp, n, n)
    new_buf = jnp.concatenate([conv_buf[:, 1:], xr[0][:, None, :]], axis=1)
    state = (k[0].reshape(n, 1, H_A, HEAD_DIM), v[0].reshape(n, 1, H_A, HEAD_DIM), gv[0][:, None, :],
             new_buf, h_last)
    return x, state


def kernel(x_prompt, x_sample, cache_k, cache_v, state_conv, state_h, c_prompt, c_sample, rel_bias, w_ada, b_ada, norm_mix, norm_ffn, w_in, q_gain, k_gain, gmlp_norm, w_s, b_s, conv_w, conv_b, w_a, b_a, w_x, b_x, lru_lambda, out_gain, w_out, peer_wq, peer_keys, peer_u, peer_v):
    depth = w_in.shape[0]
    nb, ns = x_prompt.shape[0], x_sample.shape[0]
    assert x_sample.shape[1] == 1
    pad = -(nb + ns) % 8
    c_all = jnp.concatenate([c_prompt, c_sample, jnp.zeros((pad, D_MODEL), F32)], axis=0)
    mods = _ada(c_all, w_ada, b_ada)
    biases = [_prompt_bias(rel_bias, window, dil) for window, dil in DILATED_PAIRS]
    xp = x_prompt
    xs = x_sample.reshape(1, ns, D_MODEL)
    st_p, st_s = [], []
    for l in range(depth):
        lp = dict(norm_mix=norm_mix[l], norm_ffn=norm_ffn[l], w_in=w_in[l], q_gain=q_gain[l], k_gain=k_gain[l],
                  gmlp_norm=gmlp_norm[l], w_s=w_s[l], b_s=b_s[l], conv_w=conv_w[l], conv_b=conv_b[l],
                  w_a=w_a[l], b_a=b_a[l], w_x=w_x[l], b_x=b_x[l], lru_lambda=lru_lambda[l],
                  out_gain=out_gain[l], w_out=w_out[l], peer_wq=peer_wq[l], peer_keys=peer_keys[l],
                  peer_u=peer_u[l], peer_v=peer_v[l])
        mod_p = [m[:, None, :] for m in jnp.split(mods[l, :nb], 6, axis=-1)]
        mod_s = [m[None] for m in jnp.split(mods[l, nb:nb + ns], 6, axis=-1)]
        xp, sp = _layer_prompt(xp, mod_p, lp, biases)
        xs, ss = _layer_sample(xs, mod_s, lp, rel_bias, cache_k, cache_v, state_conv[l], state_h[l], l)
        st_p.append(sp)
        st_s.append(ss)
    stack = lambda sts, i: jnp.stack([s[i] for s in sts])
    return (xp, xs.reshape(ns, 1, D_MODEL),
            stack(st_p, 0), stack(st_p, 1), stack(st_s, 0), stack(st_s, 1),
            stack(st_p, 2), stack(st_s, 2), stack(st_p, 3), stack(st_s, 3),
            stack(st_p, 4), stack(st_s, 4))
```

```python
import functools
import math

import jax
import jax.numpy as jnp
import numpy as np
from jax import lax
from jax.experimental import pallas as pl
from jax.experimental.pallas import tpu as pltpu

F32 = jnp.float32
BF16 = jnp.bfloat16
HIGHEST = lax.Precision.HIGHEST

D_MODEL = 1024
HEAD_DIM = 64
D_A = 512
H_A = 8
DILATED_PAIRS = ((128, 1), (512, 4), (2048, 16))
W_MAX = 2048
BLK = 128
ATTN_SCALE = HEAD_DIM ** -0.5
N_BUCKETS = 32
MAX_DIST = 2048
D_B = 256
G_B = 4
E_B = 64
CHUNK = 128
D_C = 256
H_C = 4
E_C = 64
CONV_W = 4
LRU_C = 8.0
PEER_HEADS = 8
N_KEYS = 128
PEER_TOPK = 16
D_QUERY = 256
D_IN = 3 * D_A + 2 * D_B + 2 * D_C
EPS = 1e-6
NEG = -1e30

VMEM_LIMIT_BYTES = 56 * 1024 * 1024

TM_PROJ = 512
LANES = 128
TT_LRU = 512
TN_SELECT = 256
TN_DENSE = 512
TE_DENSE = 512
DENSE_SLAB = 16
ANCHOR_V_SLABS = 16
ANCHOR_U_SLABS = 4


def _params(*sem):
    return pltpu.CompilerParams(dimension_semantics=sem, vmem_limit_bytes=VMEM_LIMIT_BYTES)


def _split_dot(x, w_bf16):
    hi = x.astype(BF16)
    lo = (x - hi.astype(F32)).astype(BF16)
    return (jnp.dot(hi, w_bf16, preferred_element_type=F32)
            + jnp.dot(lo, w_bf16, preferred_element_type=F32))


def _rms_rows(x):
    return x * lax.rsqrt(jnp.mean(x * x, axis=-1, keepdims=True) + EPS)


def _ada_kernel(c_ref, w_ref, b_ref, o_ref):
    c = c_ref[...]
    o_ref[...] = jnp.dot(jax.nn.silu(c), w_ref[...], precision=HIGHEST,
                         preferred_element_type=F32) + b_ref[...]


def _ada(c_all, w_ada, b_ada):
    depth, _, n_out = w_ada.shape
    rows = c_all.shape[0]
    tn = 512
    return pl.pallas_call(
        _ada_kernel,
        grid=(depth, n_out // tn),
        in_specs=[pl.BlockSpec((rows, D_MODEL), lambda l, j: (0, 0)),
                  pl.BlockSpec((None, D_MODEL, tn), lambda l, j: (l, 0, j)),
                  pl.BlockSpec((None, 1, tn), lambda l, j: (l, 0, j))],
        out_specs=pl.BlockSpec((None, rows, tn), lambda l, j: (l, 0, j)),
        out_shape=jax.ShapeDtypeStruct((depth, rows, n_out), F32),
        compiler_params=_params("parallel", "parallel"),
        name="ada",
    )(c_all, w_ada, b_ada.reshape(depth, 1, n_out))


def _inproj_kernel(x_ref, sh_ref, sc_ref, nm_ref, w_ref, qg_ref, kg_ref, gn_ref, ones_ref,
                   gw_ref, gb_ref, q_ref, k_ref, v_ref, yb_ref, gv_ref, xr_ref, xg_ref, *, chunked):
    x = x_ref[...]
    h = _rms_rows(x) * nm_ref[...] * (1.0 + sc_ref[...]) + sh_ref[...]
    z = jnp.dot(h.astype(BF16), w_ref[...], preferred_element_type=F32)
    ones = ones_ref[...]

    def headnorm(a, g):
        ss = _split_dot(a * a, ones)
        return a * lax.rsqrt(ss * (1.0 / HEAD_DIM) + EPS) * g

    q_ref[...] = headnorm(z[:, 0:D_A], qg_ref[...]) * ATTN_SCALE
    k_ref[...] = headnorm(z[:, D_A:2 * D_A], kg_ref[...])
    v_ref[...] = z[:, 2 * D_A:3 * D_A]
    o = 3 * D_A
    ub = z[:, o:o + D_B]
    vbn = _rms_rows(z[:, o + D_B:o + 2 * D_B]) * gn_ref[...]
    gv_ref[...] = vbn
    xr_ref[...] = z[:, o + 2 * D_B:o + 2 * D_B + D_C]
    xg_ref[...] = z[:, o + 2 * D_B + D_C:o + 2 * D_B + 2 * D_C]
    if chunked:
        rg = lax.broadcasted_iota(jnp.int32, (G_B * CHUNK, D_B), 0) // CHUNK
        cg = lax.broadcasted_iota(jnp.int32, (G_B * CHUNK, D_B), 1) // E_B
        keep = rg == cg
        for ci in range(x.shape[0] // CHUNK):
            rows = slice(ci * CHUNK, (ci + 1) * CHUNK)
            vc = vbn[rows]
            vbd = jnp.where(keep, jnp.concatenate([vc] * G_B, axis=0), 0.0)
            mix = _split_dot_rhs(gw_ref[0], gw_ref[1], vbd) + gb_ref[...]
            yb_ref[rows, :] = ub[rows] * mix
    else:
        yb_ref[...] = ub * (gw_ref[...] * vbn + gb_ref[...])


def _split_dot_rhs(w_hi, w_lo, x):
    x_hi = x.astype(BF16)
    x_lo = (x - x_hi.astype(F32)).astype(BF16)
    return (jnp.dot(w_hi, x_hi, preferred_element_type=F32)
            + jnp.dot(w_hi, x_lo, preferred_element_type=F32)
            + jnp.dot(w_lo, x_hi, preferred_element_type=F32))


def _bf16_pair(w):
    hi = w.astype(BF16)
    lo = (w - hi.astype(F32)).astype(BF16)
    return jnp.stack([hi, lo])


def _inproj(x, sh, sc, lp, *, chunked):
    g, t, _ = x.shape
    tm = min(TM_PROJ, t)
    r = sh.shape[1]
    mod_spec = pl.BlockSpec((None, r, D_MODEL), (lambda b, i: (b, 0, 0)))
    full = lambda a: pl.BlockSpec(a.shape, lambda b, i: (0,) * a.ndim)
    if chunked:
        gw = _bf16_pair(jnp.transpose(jnp.tril(lp['w_s']), (1, 0, 2)).reshape(CHUNK, G_B * CHUNK))
        gb = jnp.repeat(lp['b_s'].T, E_B, axis=1)
    else:
        gw = jnp.repeat(lp['w_s'][:, 0, 0], E_B)[None, :]
        gb = jnp.repeat(lp['b_s'][:, 0], E_B)[None, :]
    weights = [lp['norm_mix'][None, :], lp['w_in'].astype(BF16),
               jnp.tile(lp['q_gain'], H_A)[None, :], jnp.tile(lp['k_gain'], H_A)[None, :],
               lp['gmlp_norm'][None, :], _head_ones(), gw, gb]
    row = lambda d: pl.BlockSpec((None, tm, d), lambda b, i: (b, i, 0))
    widths = (D_A, D_A, D_A, D_B, D_B, D_C, D_C)
    return pl.pallas_call(
        functools.partial(_inproj_kernel, chunked=chunked),
        grid=(g, t // tm),
        in_specs=[row(D_MODEL), mod_spec, mod_spec] + [full(w) for w in weights],
        out_specs=[row(d) for d in widths],
        out_shape=[jax.ShapeDtypeStruct((g, t, d), F32) for d in widths],
        compiler_params=_params("parallel", "parallel"),
        name="inproj_prompt" if chunked else "inproj_sample",
    )(x, sh, sc, *weights)


def _head_ones():
    hid = np.arange(D_A) // HEAD_DIM
    return jnp.asarray((hid[:, None] == hid[None, :]).astype(np.float32), dtype=BF16)


def _t5_bucket(dist):
    exact = N_BUCKETS // 2
    df = np.maximum(dist, 1).astype(np.float64)
    large = exact + (np.log(df / exact) / math.log(MAX_DIST / exact) * (N_BUCKETS - exact)).astype(np.int64)
    return np.where(dist < exact, dist, np.minimum(large, N_BUCKETS - 1))


def _bias_lookup(rel_bias, dist):
    onehot = (_t5_bucket(dist)[..., None] == np.arange(N_BUCKETS)).astype(np.float32)
    return jnp.einsum('...b,bh->...h', jnp.asarray(onehot), rel_bias.astype(F32), precision=HIGHEST)


def _prompt_bias(rel_bias, window, dil):
    qi = np.arange(BLK)[:, None] + BLK
    ki = np.arange(2 * BLK)[None, :]
    dm = qi - ki
    valid = (dm >= 0) & (dm <= window // dil)
    bias = jnp.transpose(_bias_lookup(rel_bias, np.maximum(dm, 0) * dil), (2, 0, 1))
    return jnp.where(jnp.asarray(valid)[None], bias, NEG)


def _attn_prompt_kernel(q_ref, kp_ref, kc_ref, vp_ref, vc_ref, bias_ref, o_ref, lse_ref, o_s, l_s, *, dil):
    heads = q_ref.shape[-1] // HEAD_DIM
    h0 = pl.program_id(2) * heads
    col = lax.broadcasted_iota(jnp.int32, (BLK, 2 * BLK), 1)
    no_prev = jnp.logical_and(pl.program_id(1) == 0, col < BLK)

    def phase(p, carry):
        rows = pl.ds(p, BLK, stride=dil) if dil > 1 else slice(None)
        q = q_ref[rows, :].astype(BF16)
        k2 = jnp.concatenate([kp_ref[rows, :], kc_ref[rows, :]], axis=0).astype(BF16)
        v2 = jnp.concatenate([vp_ref[rows, :], vc_ref[rows, :]], axis=0).astype(BF16)
        for h in range(heads):
            hs = slice(h * HEAD_DIM, (h + 1) * HEAD_DIM)
            lg = lax.dot_general(q[:, hs], k2[:, hs], (((1,), (1,)), ((), ())),
                                 preferred_element_type=F32) + bias_ref[h0 + h]
            lg = jnp.where(no_prev, NEG, lg)
            m = jnp.max(lg, axis=-1, keepdims=True)
            pr = jnp.exp(lg - m)
            s = jnp.sum(pr, axis=-1, keepdims=True)
            o_s[:, hs] = jnp.dot(pr.astype(BF16), v2[:, hs], preferred_element_type=F32) / s
            l_s[:, hs] = jnp.broadcast_to(m + jnp.log(s), (BLK, HEAD_DIM))
        o_ref[rows, :] = o_s[...]
        lse_ref[rows, :] = l_s[...]
        return carry

    if dil == 1:
        phase(0, 0)
    else:
        lax.fori_loop(0, dil, phase, 0, unroll=H_A // heads)


def _attn_prompt(q, k, v, bias, dil):
    b, s, _ = q.shape
    span = BLK * dil
    heads = H_A if dil == 1 else LANES // HEAD_DIM
    assert s % span == 0 and H_A % heads == 0
    width = heads * HEAD_DIM
    cur = pl.BlockSpec((None, span, width), lambda i, n, g: (i, n, g))
    prev = pl.BlockSpec((None, span, width), lambda i, n, g: (i, jnp.maximum(n - 1, 0), g))
    return pl.pallas_call(
        functools.partial(_attn_prompt_kernel, dil=dil),
        grid=(b, s // span, H_A // heads),
        in_specs=[cur, prev, cur, prev, cur,
                  pl.BlockSpec(bias.shape, lambda i, n, g: (0, 0, 0))],
        out_specs=[cur, cur],
        out_shape=[jax.ShapeDtypeStruct((b, s, D_A), F32)] * 2,
        scratch_shapes=[pltpu.VMEM((BLK, width), F32), pltpu.VMEM((BLK, width), F32)],
        compiler_params=_params("parallel", "arbitrary", "arbitrary"),
        name=f"attn_prompt_d{dil}",
    )(q, k, k, v, v, bias)


def _attn_sample_kernel(q_ref, kn_ref, qt_ref, vnt_ref, b0_ref, kt_ref, vt_ref, br1_ref, br4_ref, br16_ref, ot_ref):
    qt = qt_ref[0]
    vnt = vnt_ref[0]
    w_cache = kt_ref.shape[-1]
    sub = lax.broadcasted_iota(jnp.int32, (H_A, w_cache), 0)
    logits = jnp.zeros((H_A, w_cache), F32)
    for h in range(H_A):
        row = jnp.sum(kt_ref[0, h] * qt[:, h:h + 1], axis=0, keepdims=True)
        logits = jnp.where(sub == h, row, logits)
    l0 = jnp.sum(q_ref[0] * kn_ref[0], axis=1, keepdims=True) + b0_ref[...]
    ps, p0s, ss, lses = [], [], [], []
    for br_ref in (br1_ref, br4_ref, br16_ref):
        span = br_ref.shape[-1]
        lg = logits[:, w_cache - span:] + br_ref[...]
        m = jnp.maximum(jnp.max(lg, axis=1, keepdims=True), l0)
        p = jnp.exp(lg - m)
        p0 = jnp.exp(l0 - m)
        s = jnp.sum(p, axis=1, keepdims=True) + p0
        ps.append(p), p0s.append(p0), ss.append(s), lses.append(m + jnp.log(s))
    mm = jnp.maximum(jnp.maximum(lses[0], lses[1]), lses[2])
    ws = [jnp.exp(l - mm) for l in lses]
    tot = ws[0] + ws[1] + ws[2]
    cs = [w / (s * tot) for w, s in zip(ws, ss)]
    p1, p4, p16 = (c * p for c, p in zip(cs, ps))
    s1, s4 = p1.shape[1], p4.shape[1]
    probs = jnp.concatenate([p16[:, :w_cache - s4],
                             p16[:, w_cache - s4:w_cache - s1] + p4[:, :s4 - s1],
                             p16[:, w_cache - s1:] + p4[:, s4 - s1:] + p1], axis=1)
    prob0 = cs[0] * p0s[0] + cs[1] * p0s[1] + cs[2] * p0s[2]
    for h in range(H_A):
        pv = jnp.sum(vt_ref[0, h] * probs[h:h + 1, :], axis=1, keepdims=True)
        ot_ref[0, :, h:h + 1] = pv + prob0[h:h + 1, :] * vnt[:, h:h + 1]


def _attn_sample(q, kn, vn, cache_k, cache_v, layer, rel_bias):
    nseq = q.shape[0]
    w_cache = cache_k.shape[2]
    heads = lambda a: a.reshape(nseq, H_A, HEAD_DIM)
    cols = lambda a: jnp.transpose(heads(a), (0, 2, 1))
    spans = [window for window, _ in DILATED_PAIRS]
    assert spans == sorted(spans) and w_cache == spans[-1]
    brs = []
    for window, dil in DILATED_PAIRS:
        assert window // dil == BLK
        dist = window - np.arange(window)
        brs.append(jnp.where(jnp.asarray(dist % dil == 0)[None], _bias_lookup(rel_bias, dist).T, NEG))
    b0 = rel_bias[0].astype(F32)[:, None]
    row = pl.BlockSpec((1, H_A, HEAD_DIM), lambda i: (i, 0, 0))
    col = pl.BlockSpec((1, HEAD_DIM, H_A), lambda i: (i, 0, 0))
    cache = pl.BlockSpec((None, 1, H_A, HEAD_DIM, w_cache), lambda i: (layer, i, 0, 0, 0))
    full = lambda a: pl.BlockSpec(a.shape, lambda i: (0,) * a.ndim)
    out = pl.pallas_call(
        _attn_sample_kernel,
        grid=(nseq,),
        in_specs=[row, row, col, col, full(b0), cache, cache] + [full(a) for a in brs],
        out_specs=col,
        out_shape=jax.ShapeDtypeStruct((nseq, HEAD_DIM, H_A), F32),
        compiler_params=_params("parallel"),
        name="attn_sample",
    )(heads(q), heads(kn), cols(q), cols(vn), b0,
      jnp.transpose(cache_k, (0, 1, 3, 4, 2)), jnp.transpose(cache_v, (0, 1, 3, 4, 2)), *brs)
    return jnp.transpose(out, (0, 2, 1)).reshape(nseq, D_A)


def _lru_gates(xc, wa_ref, ba_ref, wx_ref, bx_ref, lam_ref):
    r = jax.nn.sigmoid(jnp.dot(xc, wa_ref[...], precision=HIGHEST, preferred_element_type=F32) + ba_ref[...])
    ig = jax.nn.sigmoid(jnp.dot(xc, wx_ref[...], precision=HIGHEST, preferred_element_type=F32) + bx_ref[...])
    nl = -lam_ref[...]
    softplus = jnp.maximum(nl, 0.0) + jnp.log(1.0 + jnp.exp(-jnp.abs(nl)))
    log_a = -LRU_C * r * softplus
    a = jnp.exp(log_a)
    bterm = jnp.sqrt(1.0 - a * a) * ig * xc
    return a, bterm


def _lru_prompt_kernel(xr_ref, xg_ref, cb_ref, h0_ref, cw_ref, cbias_ref, wa_ref, ba_ref, wx_ref, bx_ref,
                       lam_ref, y_ref, nb_ref, hl_ref, xbuf, a_s, b_s, h_s, hcar):
    tt = xr_ref.shape[0]
    pad = 8

    @pl.when(pl.program_id(1) == 0)
    def _():
        xbuf[0:pad, :] = jnp.zeros((pad, D_C), F32)
        xbuf[pad - (CONV_W - 1):pad, :] = cb_ref[...]
        hcar[...] = h0_ref[...]

    xbuf[pad:pad + tt, :] = xr_ref[...]
    xc = cbias_ref[...]
    for kk in range(CONV_W):
        off = pad - (CONV_W - 1) + kk
        xc = xc + xbuf[off:off + tt, :] * cw_ref[kk:kk + 1, :]
    a, bterm = _lru_gates(xc, wa_ref, ba_ref, wx_ref, bx_ref, lam_ref)
    a_s[...] = a
    b_s[...] = bterm

    def step(i, h):
        h = a_s[pl.ds(i, 1), :] * h + b_s[pl.ds(i, 1), :]
        h_s[pl.ds(i, 1), :] = h
        return h

    h = lax.fori_loop(0, tt, step, hcar[...], unroll=8)
    hcar[...] = h
    hl_ref[...] = h
    y_ref[...] = jax.nn.gelu(xg_ref[...]) * h_s[...]
    tail = xbuf[pad + tt - (CONV_W - 1):pad + tt, :]
    nb_ref[...] = tail
    xbuf[pad - (CONV_W - 1):pad, :] = tail


def _lru_weights(lp):
    def blockdiag(w):
        eye = jnp.eye(H_C, dtype=w.dtype)
        return jnp.einsum('hij,hg->higj', w, eye).reshape(D_C, D_C)
    return [lp['conv_w'], lp['conv_b'][None, :], blockdiag(lp['w_a']), lp['b_a'][None, :],
            blockdiag(lp['w_x']), lp['b_x'][None, :], lp['lru_lambda'][None, :]]


def _lru_prompt(xr, xg, conv_buf, h0, lp):
    b, t, _ = xr.shape
    tt = min(TT_LRU, t)
    assert t % tt == 0 and tt >= CONV_W - 1
    weights = _lru_weights(lp)
    row = pl.BlockSpec((None, tt, D_C), lambda i, j: (i, j, 0))
    per_seq = lambda n: pl.BlockSpec((None, n, D_C), lambda i, j: (i, 0, 0))
    full = lambda a: pl.BlockSpec(a.shape, lambda i, j: (0,) * a.ndim)
    y, nb, hl = pl.pallas_call(
        _lru_prompt_kernel,
        grid=(b, t // tt),
        in_specs=[row, row, per_seq(CONV_W - 1), per_seq(1)] + [full(w) for w in weights],
        out_specs=[row, per_seq(CONV_W - 1), per_seq(1)],
        out_shape=[jax.ShapeDtypeStruct((b, t, D_C), F32),
                   jax.ShapeDtypeStruct((b, CONV_W - 1, D_C), F32),
                   jax.ShapeDtypeStruct((b, 1, D_C), F32)],
        scratch_shapes=[pltpu.VMEM((tt + 8, D_C), F32), pltpu.VMEM((tt, D_C), F32),
                        pltpu.VMEM((tt, D_C), F32), pltpu.VMEM((tt, D_C), F32),
                        pltpu.VMEM((1, D_C), F32)],
        compiler_params=_params("parallel", "arbitrary"),
        name="lru_prompt",
    )(xr, xg, conv_buf, h0[:, None, :], *weights)
    return y, nb, hl[:, 0, :]


def _lru_sample_kernel(xr_ref, xg_ref, c0_ref, c1_ref, c2_ref, h0_ref, cw_ref, cbias_ref, wa_ref, ba_ref,
                       wx_ref, bx_ref, lam_ref, y_ref, hl_ref):
    xc = (cbias_ref[...] + c0_ref[...] * cw_ref[0:1, :] + c1_ref[...] * cw_ref[1:2, :]
          + c2_ref[...] * cw_ref[2:3, :] + xr_ref[...] * cw_ref[3:4, :])
    a, bterm = _lru_gates(xc, wa_ref, ba_ref, wx_ref, bx_ref, lam_ref)
    h = a * h0_ref[...] + bterm
    hl_ref[...] = h
    y_ref[...] = jax.nn.gelu(xg_ref[...]) * h


def _lru_sample(xr, xg, conv_buf, h0, lp):
    n = xr.shape[0]
    weights = _lru_weights(lp)
    args = [xr, xg, conv_buf[:, 0], conv_buf[:, 1], conv_buf[:, 2], h0] + weights
    full = lambda a: pl.BlockSpec(a.shape, lambda i: (0,) * a.ndim)
    return pl.pallas_call(
        _lru_sample_kernel,
        grid=(1,),
        in_specs=[full(a) for a in args],
        out_specs=[full(xr), full(xr)],
        out_shape=[jax.ShapeDtypeStruct((n, D_C), F32)] * 2,
        compiler_params=_params("arbitrary"),
        name="lru_sample",
    )(*args)


def _outproj_kernel(*refs, merge):
    if merge:
        o1, o2, o3, l1, l2, l3 = refs[:6]
        refs = refs[6:]
        lses = [l1[...], l2[...], l3[...]]
        mm = jnp.maximum(jnp.maximum(lses[0], lses[1]), lses[2])
        ws = [jnp.exp(l - mm) for l in lses]
        ya = (ws[0] * o1[...] + ws[1] * o2[...] + ws[2] * o3[...]) / (ws[0] + ws[1] + ws[2])
    else:
        ya = refs[0][...]
        refs = refs[1:]
    yb_ref, yc_ref, x_ref, g1_ref, sh_ref, sc_ref, og_ref, w_ref, nf_ref, xm_ref, h2_ref = refs
    og = og_ref[...]
    mix = jnp.concatenate([_rms_rows(ya) * og[:, 0:D_A],
                           _rms_rows(yb_ref[...]) * og[:, D_A:D_A + D_B],
                           _rms_rows(yc_ref[...]) * og[:, D_A + D_B:]], axis=-1)
    x = x_ref[...] + g1_ref[...] * jnp.dot(mix.astype(BF16), w_ref[...], preferred_element_type=F32)
    xm_ref[...] = x
    h2 = _rms_rows(x) * nf_ref[...] * (1.0 + sc_ref[...]) + sh_ref[...]
    h2_ref[...] = h2.astype(BF16)


def _outproj(attn, yb, yc, x, g1, sh2, sc2, lp):
    g, t, _ = x.shape
    tm = min(TM_PROJ, t)
    merge = len(attn) > 1
    r = g1.shape[1]
    row = lambda d: pl.BlockSpec((None, tm, d), lambda b, i: (b, i, 0))
    mod_spec = pl.BlockSpec((None, r, D_MODEL), lambda b, i: (b, 0, 0))
    full = lambda a: pl.BlockSpec(a.shape, lambda b, i: (0,) * a.ndim)
    weights = [lp['out_gain'][None, :], lp['w_out'].astype(BF16), lp['norm_ffn'][None, :]]
    return pl.pallas_call(
        functools.partial(_outproj_kernel, merge=merge),
        grid=(g, t // tm),
        in_specs=[row(D_A)] * len(attn) + [row(D_B), row(D_C), row(D_MODEL), mod_spec, mod_spec, mod_spec]
                 + [full(w) for w in weights],
        out_specs=[row(D_MODEL), row(D_MODEL)],
        out_shape=[jax.ShapeDtypeStruct((g, t, D_MODEL), F32), jax.ShapeDtypeStruct((g, t, D_MODEL), BF16)],
        compiler_params=_params("parallel", "parallel"),
        name="outproj_prompt" if merge else "outproj_sample",
    )(*attn, yb, yc, x, g1, sh2, sc2, *weights)


def _sort16_pairs():
    pairs = []

    def merge(lo, hi, r):
        step = r * 2
        if step < hi - lo:
            merge(lo, hi, step)
            merge(lo + r, hi, step)
            pairs.extend((i, i + r) for i in range(lo + r, hi - r, step))
        else:
            pairs.append((lo, lo + r))

    def sort(lo, hi):
        if hi - lo >= 1:
            mid = lo + (hi - lo) // 2
            sort(lo, mid)
            sort(mid + 1, hi)
            merge(lo, hi, 1)

    sort(0, PEER_TOPK - 1)
    return pairs


_SORT16 = _sort16_pairs()
_BITONIC16 = [(i, i + d) for d in (8, 4, 2, 1) for i in range(PEER_TOPK) if not i & d]


def _exchange(vals, pairs):
    vals = list(vals)
    for i, j in pairs:
        hi, lo = jnp.maximum(vals[i], vals[j]), jnp.minimum(vals[i], vals[j])
        vals[i], vals[j] = hi, lo
    return vals


def _merge_top16(a, b):
    return _exchange([jnp.maximum(a[i], b[PEER_TOPK - 1 - i]) for i in range(PEER_TOPK)], _BITONIC16)


def _across_sublanes(vals):
    for shift in (4, 2, 1):
        vals = _merge_top16(vals, [pltpu.roll(v, shift, axis=0) for v in vals])
    return vals


def _top16_of_keys(s):
    slabs = [s[8 * i:8 * i + 8, :] for i in range(N_KEYS // 8)]
    return _across_sublanes(_exchange(slabs, _SORT16))


def _prefix_count(holds, vals):
    pick = jnp.where
    m8 = holds(vals[7])
    m4 = holds(pick(m8, vals[11], vals[3]))
    m2 = holds(pick(m8, pick(m4, vals[13], vals[9]), pick(m4, vals[5], vals[1])))
    m1 = holds(pick(m8, pick(m4, pick(m2, vals[14], vals[12]), pick(m2, vals[10], vals[8])),
                    pick(m4, pick(m2, vals[6], vals[4]), pick(m2, vals[2], vals[0]))))
    m0 = holds(vals[15])
    return (pick(m8, 8.0, 0.0) + pick(m4, 4.0, 0.0) + pick(m2, 2.0, 0.0) + pick(m1, 1.0, 0.0)
            + pick(m0, 1.0, 0.0))


def _peer_select_kernel(xt_ref, wq_ref, keys_ref, cnt_ref, e1_ref, rk_ref, e2_ref, q_s):
    tn = xt_ref.shape[1]
    sub = lax.broadcasted_iota(jnp.int32, (8, tn), 0)
    q_s[...] = jnp.dot(wq_ref[...], xt_ref[...], preferred_element_type=F32)
    half = D_QUERY // 2

    def head(h, carry):
        s = []
        for p in range(2):
            qt = q_s[pl.ds(pl.multiple_of((2 * h + p) * half, half), half), :]
            qn = qt * lax.rsqrt(jnp.mean(qt * qt, axis=0, keepdims=True) + EPS)
            s.append(_split_dot_rhs(keys_ref[0, 2 * h + p], keys_ref[1, 2 * h + p], qn))
        s1, s2 = s
        v1 = _top16_of_keys(s1)
        v2 = _top16_of_keys(s2)
        a_lo, a_hi = v1[0], v1[8]
        for j in range(1, 8):
            a_lo = jnp.where(sub == j, v1[j], a_lo)
            a_hi = jnp.where(sub == j, v1[8 + j], a_hi)
        sv = _across_sublanes(_merge_top16([a_lo + b for b in v2], [a_hi + b for b in v2]))
        tau = sv[PEER_TOPK - 1][None]
        z = jnp.ones_like(sv[0])
        for kk in range(1, PEER_TOPK):
            z = z + jnp.exp(sv[kk] - sv[0])
        inv_z = 1.0 / z
        for i in range(N_KEYS // 16):
            rows = slice(16 * i, 16 * i + 16)
            s1r = s1[rows].reshape(2, 8, tn)
            s2r = s2[rows].reshape(2, 8, tn)
            cnt = _prefix_count(lambda v: s1r + v >= tau, v2)
            rk = _prefix_count(lambda v: v > s2r, v2)
            cnt_ref[h, rows, :] = cnt.reshape(16, tn)
            e1_ref[h, rows, :] = jnp.exp(s1r - v1[0][None]).reshape(16, tn)
            rk_ref[h, rows, :] = rk.reshape(16, tn).astype(BF16)
            e2_ref[h, rows, :] = (jnp.exp(s2r - v2[0][None]) * inv_z[None]).reshape(16, tn).astype(BF16)
        return carry

    lax.fori_loop(0, PEER_HEADS, head, 0)


def _peer_select(h2t, lp, tn):
    n = h2t.shape[1]
    wq = lp['peer_wq'].T.astype(BF16)
    keys = _bf16_pair(lp['peer_keys'].reshape(2 * PEER_HEADS, N_KEYS, D_QUERY // 2))
    out = pl.BlockSpec((PEER_HEADS, N_KEYS, tn), lambda i: (0, 0, i))
    shape = lambda dt: jax.ShapeDtypeStruct((PEER_HEADS, N_KEYS, n), dt)
    return pl.pallas_call(
        _peer_select_kernel,
        grid=(n // tn,),
        in_specs=[pl.BlockSpec((D_MODEL, tn), lambda i: (0, i)),
                  pl.BlockSpec(wq.shape, lambda i: (0, 0)),
                  pl.BlockSpec(keys.shape, lambda i: (0, 0, 0, 0))],
        out_specs=[out] * 4,
        out_shape=[shape(F32), shape(F32), shape(BF16), shape(BF16)],
        scratch_shapes=[pltpu.VMEM((PEER_HEADS * D_QUERY, tn), F32)],
        compiler_params=_params("parallel"),
        name="peer_select",
    )(h2t, wq, keys)


def _peer_dense_kernel(xt_ref, u0_ref, u_ref, vt_ref, cnt_ref, e1_ref, rk_ref, e2_ref, xm_ref, g2_ref, zbits_ref,
                       o_ref, acc, at_s, at_n, m_s, cntb, e1b):
    j = pl.program_id(1)
    last = pl.num_programs(1) - 1
    te, tn = at_s.shape
    n_r = te // N_KEYS
    slab = DENSE_SLAB

    @pl.when(j == 0)
    def _():
        acc[...] = jnp.zeros_like(acc)
        m_s[...] = jnp.zeros_like(m_s)
        at_s[...] = jnp.dot(u0_ref[...], xt_ref[...], preferred_element_type=F32)

    fed = jnp.dot(vt_ref[...], m_s[...], preferred_element_type=F32)
    acc[...] += fed
    following = jnp.dot(u_ref[...], xt_ref[...], preferred_element_type=F32)
    at_n[...] = following

    def zero_after(x):
        return pltpu.bitcast(pltpu.bitcast(x[-8:, :], jnp.int32) & zbits_ref[...], F32)[0:1, :]

    after_v, after_u = zero_after(fed), zero_after(following)

    r0 = jnp.minimum(j, last - 1) * n_r
    for rr in range(n_r):
        for h in range(PEER_HEADS):
            cntb[rr * PEER_HEADS + h] = jnp.broadcast_to(cnt_ref[h, pl.ds(r0 + rr, 1), :], (slab, tn)).astype(BF16)
            e1b[rr * PEER_HEADS + h] = jnp.broadcast_to(e1_ref[h, pl.ds(r0 + rr, 1), :], (slab, tn)).astype(BF16)
    n_slabs = te // slab
    for s in range(n_slabs):
        rr = s * slab // N_KEYS
        keys2 = slice(s * slab % N_KEYS, s * slab % N_KEYS + slab)
        experts = slice(s * slab, (s + 1) * slab)
        gate = jnp.zeros((slab, tn), BF16)
        for h in range(PEER_HEADS):
            picked = rk_ref[h, keys2, :] < cntb[rr * PEER_HEADS + h]
            gate = gate + jnp.where(picked, e2_ref[h, keys2, :], 0.0) * e1b[rr * PEER_HEADS + h]
        pre = at_s[experts, :]
        if s >= n_slabs - ANCHOR_U_SLABS:
            pre = pre + after_u
        elif s >= n_slabs - ANCHOR_V_SLABS:
            pre = pre + after_v
        m_s[experts, :] = gate * jax.nn.gelu(pre).astype(BF16)

    at_s[...] = at_n[...]

    @pl.when(j == last)
    def _():
        o_ref[...] = xm_ref[...] + g2_ref[...] * acc[...].T


def _peer_dense(h2t, sel, xmid, g2, lp, tn):
    g, t, _ = xmid.shape
    n = g * t
    te = TE_DENSE
    n_exp = lp['peer_u'].shape[0]
    nj = n_exp // te
    u = lp['peer_u'].astype(BF16)
    vt = lp['peer_v'].T.astype(BF16)
    tiles_per_seq = t // tn
    r = g2.shape[1]
    tok = pl.BlockSpec((PEER_HEADS, N_KEYS, tn), lambda i, j: (0, 0, i))
    row = pl.BlockSpec((None, tn, D_MODEL), lambda i, j: (i // tiles_per_seq, i % tiles_per_seq, 0))
    return pl.pallas_call(
        _peer_dense_kernel,
        grid=(n // tn, nj + 1),
        in_specs=[pl.BlockSpec((D_MODEL, tn), lambda i, j: (0, i)),
                  pl.BlockSpec((te, D_MODEL), lambda i, j: (0, 0)),
                  pl.BlockSpec((te, D_MODEL), lambda i, j: (jnp.minimum(j + 1, nj - 1), 0)),
                  pl.BlockSpec((D_MODEL, te), lambda i, j: (0, jnp.maximum(j - 1, 0))),
                  tok, tok, tok, tok, row,
                  pl.BlockSpec((None, r, D_MODEL), lambda i, j: (i // tiles_per_seq, 0, 0)),
                  pl.BlockSpec((8, tn), lambda i, j: (0, 0))],
        out_specs=row,
        out_shape=jax.ShapeDtypeStruct((g, t, D_MODEL), F32),
        scratch_shapes=[pltpu.VMEM((D_MODEL, tn), F32), pltpu.VMEM((te, tn), F32), pltpu.VMEM((te, tn), F32),
                        pltpu.VMEM((te, tn), BF16),
                        pltpu.VMEM((te // N_KEYS * PEER_HEADS, DENSE_SLAB, tn), BF16),
                        pltpu.VMEM((te // N_KEYS * PEER_HEADS, DENSE_SLAB, tn), BF16)],
        compiler_params=_params("parallel", "arbitrary"),
        name="peer_dense",
    )(h2t, u, u, vt, *sel, xmid, g2, jnp.zeros((8, tn), jnp.int32))


def _peer(xmid, h2, g2, lp, tn_select, tn_dense):
    g, t, _ = xmid.shape
    h2t = h2.reshape(g * t, D_MODEL).T
    sel = _peer_select(h2t, lp, tn_select)
    return _peer_dense(h2t, sel, xmid, g2, lp, tn_dense)


def _layer_prompt(x, mod, lp, biases):
    sh1, sc1, g1, sh2, sc2, g2 = mod
    b, t, _ = x.shape
    q, k, v, yb, gv, xr, xg = _inproj(x, sh1, sc1, lp, chunked=True)
    branches = [_attn_prompt(q, k, v, bias, dil) for bias, (_, dil) in zip(biases, DILATED_PAIRS)]
    attn = [o for o, _ in branches] + [l for _, l in branches]
    yc, new_buf, h_last = _lru_prompt(xr, xg, jnp.zeros((b, CONV_W - 1, D_C), F32), jnp.zeros((b, D_C), F32), lp)
    xmid, h2 = _outproj(attn, yb, yc, x, g1, sh2, sc2, lp)
    x = _peer(xmid, h2, g2, lp, TN_SELECT, TN_DENSE)
    keep = min(W_MAX, t)
    state = (k[:, t - keep:].reshape(b, keep, H_A, HEAD_DIM), v[:, t - keep:].reshape(b, keep, H_A, HEAD_DIM),
             gv[:, t - CHUNK:], new_buf, h_last)
    return x, state


def _layer_sample(x, mod, lp, rel_bias, cache_k, cache_v, conv_buf, h0, layer):
    sh1, sc1, g1, sh2, sc2, g2 = mod
    _, n, _ = x.shape
    q, k, v, yb, gv, xr, xg = _inproj(x, sh1, sc1, lp, chunked=False)
    ya = _attn_sample(q[0], k[0], v[0], cache_k, cache_v, layer, rel_bias)
    yc, h_last = _lru_sample(xr[0], xg[0], conv_buf, h0, lp)
    xmid, h2 = _outproj([ya[None]], yb, yc[None], x, g1, sh2, sc2, lp)
    x = _peer(xmid, h2, g2, lp, n, n)
    new_buf = jnp.concatenate([conv_buf[:, 1:], xr[0][:, None, :]], axis=1)
    state = (k[0].reshape(n, 1, H_A, HEAD_DIM), v[0].reshape(n, 1, H_A, HEAD_DIM), gv[0][:, None, :],
             new_buf, h_last)
    return x, state


def kernel(x_prompt, x_sample, cache_k, cache_v, state_conv, state_h, c_prompt, c_sample, rel_bias, w_ada, b_ada, norm_mix, norm_ffn, w_in, q_gain, k_gain, gmlp_norm, w_s, b_s, conv_w, conv_b, w_a, b_a, w_x, b_x, lru_lambda, out_gain, w_out, peer_wq, peer_keys, peer_u, peer_v):
    depth = w_in.shape[0]
    nb, ns = x_prompt.shape[0], x_sample.shape[0]
    assert x_sample.shape[1] == 1
    pad = -(nb + ns) % 8
    c_all = jnp.concatenate([c_prompt, c_sample, jnp.zeros((pad, D_MODEL), F32)], axis=0)
    mods = _ada(c_all, w_ada, b_ada)
    biases = [_prompt_bias(rel_bias, window, dil) for window, dil in DILATED_PAIRS]
    xp = x_prompt
    xs = x_sample.reshape(1, ns, D_MODEL)
    st_p, st_s = [], []
    for l in range(depth):
        lp = dict(norm_mix=norm_mix[l], norm_ffn=norm_ffn[l], w_in=w_in[l], q_gain=q_gain[l], k_gain=k_gain[l],
                  gmlp_norm=gmlp_norm[l], w_s=w_s[l], b_s=b_s[l], conv_w=conv_w[l], conv_b=conv_b[l],
                  w_a=w_a[l], b_a=b_a[l], w_x=w_x[l], b_x=b_x[l], lru_lambda=lru_lambda[l],
                  out_gain=out_gain[l], w_out=w_out[l], peer_wq=peer_wq[l], peer_keys=peer_keys[l],
                  peer_u=peer_u[l], peer_v=peer_v[l])
        mod_p = [m[:, None, :] for m in jnp.split(mods[l, :nb], 6, axis=-1)]
        mod_s = [m[None] for m in jnp.split(mods[l, nb:nb + ns], 6, axis=-1)]
        xp, sp = _layer_prompt(xp, mod_p, lp, biases)
        xs, ss = _layer_sample(xs, mod_s, lp, rel_bias, cache_k, cache_v, state_conv[l], state_h[l], l)
        st_p.append(sp)
        st_s.append(ss)
    stack = lambda sts, i: jnp.stack([s[i] for s in sts])
    return (xp, xs.reshape(ns, 1, D_MODEL),
            stack(st_p, 0), stack(st_p, 1), stack(st_s, 0), stack(st_s, 1),
            stack(st_p, 2), stack(st_s, 2), stack(st_p, 3), stack(st_s, 3),
            stack(st_p, 4), stack(st_s, 4))
```

```python
import functools
import math

import jax
import jax.numpy as jnp
import numpy as np
from jax import lax
from jax.experimental import pallas as pl
from jax.experimental.pallas import tpu as pltpu

F32 = jnp.float32
BF16 = jnp.bfloat16
HIGHEST = lax.Precision.HIGHEST

D_MODEL = 1024
HEAD_DIM = 64
D_A = 512
H_A = 8
DILATED_PAIRS = ((128, 1), (512, 4), (2048, 16))
W_MAX = 2048
BLK = 128
ATTN_SCALE = HEAD_DIM ** -0.5
N_BUCKETS = 32
MAX_DIST = 2048
D_B = 256
G_B = 4
E_B = 64
CHUNK = 128
D_C = 256
H_C = 4
E_C = 64
CONV_W = 4
LRU_C = 8.0
PEER_HEADS = 8
N_KEYS = 128
PEER_TOPK = 16
D_QUERY = 256
D_IN = 3 * D_A + 2 * D_B + 2 * D_C
EPS = 1e-6
NEG = -1e30

VMEM_LIMIT_BYTES = 56 * 1024 * 1024

TM_PROJ = 512
LANES = 128
TT_LRU = 512
TN_SELECT = 256
TN_DENSE = 512
TE_DENSE = 512
DENSE_SLAB = 16
ANCHOR_V_SLABS = 16
ANCHOR_U_SLABS = 4


def _params(*sem):
    return pltpu.CompilerParams(dimension_semantics=sem, vmem_limit_bytes=VMEM_LIMIT_BYTES)


def _split_dot(x, w_bf16):
    hi = x.astype(BF16)
    lo = (x - hi.astype(F32)).astype(BF16)
    return (jnp.dot(hi, w_bf16, preferred_element_type=F32)
            + jnp.dot(lo, w_bf16, preferred_element_type=F32))


def _rms_rows(x):
    return x * lax.rsqrt(jnp.mean(x * x, axis=-1, keepdims=True) + EPS)


def _ada_kernel(c_ref, w_ref, b_ref, o_ref):
    c = c_ref[...]
    o_ref[...] = jnp.dot(jax.nn.silu(c), w_ref[...], precision=HIGHEST,
                         preferred_element_type=F32) + b_ref[...]


def _ada(c_all, w_ada, b_ada):
    depth, _, n_out = w_ada.shape
    rows = c_all.shape[0]
    tn = 512
    return pl.pallas_call(
        _ada_kernel,
        grid=(depth, n_out // tn),
        in_specs=[pl.BlockSpec((rows, D_MODEL), lambda l, j: (0, 0)),
                  pl.BlockSpec((None, D_MODEL, tn), lambda l, j: (l, 0, j)),
                  pl.BlockSpec((None, 1, tn), lambda l, j: (l, 0, j))],
        out_specs=pl.BlockSpec((None, rows, tn), lambda l, j: (l, 0, j)),
        out_shape=jax.ShapeDtypeStruct((depth, rows, n_out), F32),
        compiler_params=_params("parallel", "parallel"),
        name="ada",
    )(c_all, w_ada, b_ada.reshape(depth, 1, n_out))


def _inproj_kernel(x_ref, sh_ref, sc_ref, nm_ref, w_ref, qg_ref, kg_ref, gn_ref, ones_ref,
                   gw_ref, gb_ref, q_ref, k_ref, v_ref, yb_ref, gv_ref, xr_ref, xg_ref, *, chunked):
    x = x_ref[...]
    h = _rms_rows(x) * nm_ref[...] * (1.0 + sc_ref[...]) + sh_ref[...]
    z = jnp.dot(h.astype(BF16), w_ref[...], preferred_element_type=F32)
    ones = ones_ref[...]

    def headnorm(a, g):
        ss = _split_dot(a * a, ones)
        return a * lax.rsqrt(ss * (1.0 / HEAD_DIM) + EPS) * g

    q_ref[...] = headnorm(z[:, 0:D_A], qg_ref[...]) * ATTN_SCALE
    k_ref[...] = headnorm(z[:, D_A:2 * D_A], kg_ref[...])
    v_ref[...] = z[:, 2 * D_A:3 * D_A]
    o = 3 * D_A
    ub = z[:, o:o + D_B]
    vbn = _rms_rows(z[:, o + D_B:o + 2 * D_B]) * gn_ref[...]
    gv_ref[...] = vbn
    xr_ref[...] = z[:, o + 2 * D_B:o + 2 * D_B + D_C]
    xg_ref[...] = z[:, o + 2 * D_B + D_C:o + 2 * D_B + 2 * D_C]
    if chunked:
        rg = lax.broadcasted_iota(jnp.int32, (G_B * CHUNK, D_B), 0) // CHUNK
        cg = lax.broadcasted_iota(jnp.int32, (G_B * CHUNK, D_B), 1) // E_B
        keep = rg == cg
        for ci in range(x.shape[0] // CHUNK):
            rows = slice(ci * CHUNK, (ci + 1) * CHUNK)
            vc = vbn[rows]
            vbd = jnp.where(keep, jnp.concatenate([vc] * G_B, axis=0), 0.0)
            mix = _split_dot_rhs(gw_ref[0], gw_ref[1], vbd) + gb_ref[...]
            yb_ref[rows, :] = ub[rows] * mix
    else:
        yb_ref[...] = ub * (gw_ref[...] * vbn + gb_ref[...])


def _split_dot_rhs(w_hi, w_lo, x):
    x_hi = x.astype(BF16)
    x_lo = (x - x_hi.astype(F32)).astype(BF16)
    return (jnp.dot(w_hi, x_hi, preferred_element_type=F32)
            + jnp.dot(w_hi, x_lo, preferred_element_type=F32)
            + jnp.dot(w_lo, x_hi, preferred_element_type=F32))


def _bf16_pair(w):
    hi = w.astype(BF16)
    lo = (w - hi.astype(F32)).astype(BF16)
    return jnp.stack([hi, lo])


def _inproj(x, sh, sc, lp, *, chunked):
    g, t, _ = x.shape
    tm = min(TM_PROJ, t)
    r = sh.shape[1]
    mod_spec = pl.BlockSpec((None, r, D_MODEL), (lambda b, i: (b, 0, 0)))
    full = lambda a: pl.BlockSpec(a.shape, lambda b, i: (0,) * a.ndim)
    if chunked:
        gw = _bf16_pair(jnp.transpose(jnp.tril(lp['w_s']), (1, 0, 2)).reshape(CHUNK, G_B * CHUNK))
        gb = jnp.repeat(lp['b_s'].T, E_B, axis=1)
    else:
        gw = jnp.repeat(lp['w_s'][:, 0, 0], E_B)[None, :]
        gb = jnp.repeat(lp['b_s'][:, 0], E_B)[None, :]
    weights = [lp['norm_mix'][None, :], lp['w_in'].astype(BF16),
               jnp.tile(lp['q_gain'], H_A)[None, :], jnp.tile(lp['k_gain'], H_A)[None, :],
               lp['gmlp_norm'][None, :], _head_ones(), gw, gb]
    row = lambda d: pl.BlockSpec((None, tm, d), lambda b, i: (b, i, 0))
    widths = (D_A, D_A, D_A, D_B, D_B, D_C, D_C)
    return pl.pallas_call(
        functools.partial(_inproj_kernel, chunked=chunked),
        grid=(g, t // tm),
        in_specs=[row(D_MODEL), mod_spec, mod_spec] + [full(w) for w in weights],
        out_specs=[row(d) for d in widths],
        out_shape=[jax.ShapeDtypeStruct((g, t, d), F32) for d in widths],
        compiler_params=_params("parallel", "parallel"),
        name="inproj_prompt" if chunked else "inproj_sample",
    )(x, sh, sc, *weights)


def _head_ones():
    hid = np.arange(D_A) // HEAD_DIM
    return jnp.asarray((hid[:, None] == hid[None, :]).astype(np.float32), dtype=BF16)


def _t5_bucket(dist):
    exact = N_BUCKETS // 2
    df = np.maximum(dist, 1).astype(np.float64)
    large = exact + (np.log(df / exact) / math.log(MAX_DIST / exact) * (N_BUCKETS - exact)).astype(np.int64)
    return np.where(dist < exact, dist, np.minimum(large, N_BUCKETS - 1))


def _bias_lookup(rel_bias, dist):
    onehot = (_t5_bucket(dist)[..., None] == np.arange(N_BUCKETS)).astype(np.float32)
    return jnp.einsum('...b,bh->...h', jnp.asarray(onehot), rel_bias.astype(F32), precision=HIGHEST)


def _prompt_bias(rel_bias, window, dil):
    qi = np.arange(BLK)[:, None] + BLK
    ki = np.arange(2 * BLK)[None, :]
    dm = qi - ki
    valid = (dm >= 0) & (dm <= window // dil)
    bias = jnp.transpose(_bias_lookup(rel_bias, np.maximum(dm, 0) * dil), (2, 0, 1))
    return jnp.where(jnp.asarray(valid)[None], bias, NEG)


def _attn_prompt_kernel(q_ref, kp_ref, kc_ref, vp_ref, vc_ref, bias_ref, o_ref, lse_ref, o_s, l_s, *, dil):
    heads = q_ref.shape[-1] // HEAD_DIM
    h0 = pl.program_id(2) * heads
    col = lax.broadcasted_iota(jnp.int32, (BLK, 2 * BLK), 1)
    no_prev = jnp.logical_and(pl.program_id(1) == 0, col < BLK)

    def phase(p, carry):
        rows = pl.ds(p, BLK, stride=dil) if dil > 1 else slice(None)
        q = q_ref[rows, :].astype(BF16)
        k2 = jnp.concatenate([kp_ref[rows, :], kc_ref[rows, :]], axis=0).astype(BF16)
        v2 = jnp.concatenate([vp_ref[rows, :], vc_ref[rows, :]], axis=0).astype(BF16)
        for h in range(heads):
            hs = slice(h * HEAD_DIM, (h + 1) * HEAD_DIM)
            lg = lax.dot_general(q[:, hs], k2[:, hs], (((1,), (1,)), ((), ())),
                                 preferred_element_type=F32) + bias_ref[h0 + h]
            lg = jnp.where(no_prev, NEG, lg)
            m = jnp.max(lg, axis=-1, keepdims=True)
            pr = jnp.exp(lg - m)
            s = jnp.sum(pr, axis=-1, keepdims=True)
            o_s[:, hs] = jnp.dot(pr.astype(BF16), v2[:, hs], preferred_element_type=F32) / s
            l_s[:, hs] = jnp.broadcast_to(m + jnp.log(s), (BLK, HEAD_DIM))
        o_ref[rows, :] = o_s[...]
        lse_ref[rows, :] = l_s[...]
        return carry

    if dil == 1:
        phase(0, 0)
    else:
        lax.fori_loop(0, dil, phase, 0, unroll=H_A // heads)


def _attn_prompt(q, k, v, bias, dil):
    b, s, _ = q.shape
    span = BLK * dil
    heads = H_A if dil == 1 else LANES // HEAD_DIM
    assert s % span == 0 and H_A % heads == 0
    width = heads * HEAD_DIM
    cur = pl.BlockSpec((None, span, width), lambda i, n, g: (i, n, g))
    prev = pl.BlockSpec((None, span, width), lambda i, n, g: (i, jnp.maximum(n - 1, 0), g))
    return pl.pallas_call(
        functools.partial(_attn_prompt_kernel, dil=dil),
        grid=(b, s // span, H_A // heads),
        in_specs=[cur, prev, cur, prev, cur,
                  pl.BlockSpec(bias.shape, lambda i, n, g: (0, 0, 0))],
        out_specs=[cur, cur],
        out_shape=[jax.ShapeDtypeStruct((b, s, D_A), F32)] * 2,
        scratch_shapes=[pltpu.VMEM((BLK, width), F32), pltpu.VMEM((BLK, width), F32)],
        compiler_params=_params("parallel", "arbitrary", "arbitrary"),
        name=f"attn_prompt_d{dil}",
    )(q, k, k, v, v, bias)


def _attn_sample_kernel(q_ref, kn_ref, qt_ref, vnt_ref, b0_ref, kt_ref, vt_ref, br1_ref, br4_ref, br16_ref, ot_ref):
    qt = qt_ref[0]
    vnt = vnt_ref[0]
    w_cache = kt_ref.shape[-1]
    sub = lax.broadcasted_iota(jnp.int32, (H_A, w_cache), 0)
    logits = jnp.zeros((H_A, w_cache), F32)
    for h in range(H_A):
        row = jnp.sum(kt_ref[0, h] * qt[:, h:h + 1], axis=0, keepdims=True)
        logits = jnp.where(sub == h, row, logits)
    l0 = jnp.sum(q_ref[0] * kn_ref[0], axis=1, keepdims=True) + b0_ref[...]
    ps, p0s, ss, lses = [], [], [], []
    for br_ref in (br1_ref, br4_ref, br16_ref):
        span = br_ref.shape[-1]
        lg = logits[:, w_cache - span:] + br_ref[...]
        m = jnp.maximum(jnp.max(lg, axis=1, keepdims=True), l0)
        p = jnp.exp(lg - m)
        p0 = jnp.exp(l0 - m)
        s = jnp.sum(p, axis=1, keepdims=True) + p0
        ps.append(p), p0s.append(p0), ss.append(s), lses.append(m + jnp.log(s))
    mm = jnp.maximum(jnp.maximum(lses[0], lses[1]), lses[2])
    ws = [jnp.exp(l - mm) for l in lses]
    tot = ws[0] + ws[1] + ws[2]
    cs = [w / (s * tot) for w, s in zip(ws, ss)]
    p1, p4, p16 = (c * p for c, p in zip(cs, ps))
    s1, s4 = p1.shape[1], p4.shape[1]
    probs = jnp.concatenate([p16[:, :w_cache - s4],
                             p16[:, w_cache - s4:w_cache - s1] + p4[:, :s4 - s1],
                             p16[:, w_cache - s1:] + p4[:, s4 - s1:] + p1], axis=1)
    prob0 = cs[0] * p0s[0] + cs[1] * p0s[1] + cs[2] * p0s[2]
    for h in range(H_A):
        pv = jnp.sum(vt_ref[0, h] * probs[h:h + 1, :], axis=1, keepdims=True)
        ot_ref[0, :, h:h + 1] = pv + prob0[h:h + 1, :] * vnt[:, h:h + 1]


def _attn_sample(q, kn, vn, cache_k, cache_v, layer, rel_bias):
    nseq = q.shape[0]
    w_cache = cache_k.shape[2]
    heads = lambda a: a.reshape(nseq, H_A, HEAD_DIM)
    cols = lambda a: jnp.transpose(heads(a), (0, 2, 1))
    spans = [window for window, _ in DILATED_PAIRS]
    assert spans == sorted(spans) and w_cache == spans[-1]
    brs = []
    for window, dil in DILATED_PAIRS:
        assert window // dil == BLK
        dist = window - np.arange(window)
        brs.append(jnp.where(jnp.asarray(dist % dil == 0)[None], _bias_lookup(rel_bias, dist).T, NEG))
    b0 = rel_bias[0].astype(F32)[:, None]
    row = pl.BlockSpec((1, H_A, HEAD_DIM), lambda i: (i, 0, 0))
    col = pl.BlockSpec((1, HEAD_DIM, H_A), lambda i: (i, 0, 0))
    cache = pl.BlockSpec((None, 1, H_A, HEAD_DIM, w_cache), lambda i: (layer, i, 0, 0, 0))
    full = lambda a: pl.BlockSpec(a.shape, lambda i: (0,) * a.ndim)
    out = pl.pallas_call(
        _attn_sample_kernel,
        grid=(nseq,),
        in_specs=[row, row, col, col, full(b0), cache, cache] + [full(a) for a in brs],
        out_specs=col,
        out_shape=jax.ShapeDtypeStruct((nseq, HEAD_DIM, H_A), F32),
        compiler_params=_params("parallel"),
        name="attn_sample",
    )(heads(q), heads(kn), cols(q), cols(vn), b0,
      jnp.transpose(cache_k, (0, 1, 3, 4, 2)), jnp.transpose(cache_v, (0, 1, 3, 4, 2)), *brs)
    return jnp.transpose(out, (0, 2, 1)).reshape(nseq, D_A)


def _lru_gates(xc, wa_ref, ba_ref, wx_ref, bx_ref, lam_ref):
    r = jax.nn.sigmoid(jnp.dot(xc, wa_ref[...], precision=HIGHEST, preferred_element_type=F32) + ba_ref[...])
    ig = jax.nn.sigmoid(jnp.dot(xc, wx_ref[...], precision=HIGHEST, preferred_element_type=F32) + bx_ref[...])
    nl = -lam_ref[...]
    softplus = jnp.maximum(nl, 0.0) + jnp.log(1.0 + jnp.exp(-jnp.abs(nl)))
    log_a = -LRU_C * r * softplus
    a = jnp.exp(log_a)
    bterm = jnp.sqrt(1.0 - a * a) * ig * xc
    return a, bterm


def _lru_prompt_kernel(xr_ref, xg_ref, cb_ref, h0_ref, cw_ref, cbias_ref, wa_ref, ba_ref, wx_ref, bx_ref,
                       lam_ref, y_ref, nb_ref, hl_ref, xbuf, a_s, b_s, h_s, hcar):
    tt = xr_ref.shape[0]
    pad = 8

    @pl.when(pl.program_id(1) == 0)
    def _():
        xbuf[0:pad, :] = jnp.zeros((pad, D_C), F32)
        xbuf[pad - (CONV_W - 1):pad, :] = cb_ref[...]
        hcar[...] = h0_ref[...]

    xbuf[pad:pad + tt, :] = xr_ref[...]
    xc = cbias_ref[...]
    for kk in range(CONV_W):
        off = pad - (CONV_W - 1) + kk
        xc = xc + xbuf[off:off + tt, :] * cw_ref[kk:kk + 1, :]
    a, bterm = _lru_gates(xc, wa_ref, ba_ref, wx_ref, bx_ref, lam_ref)
    a_s[...] = a
    b_s[...] = bterm

    def step(i, h):
        h = a_s[pl.ds(i, 1), :] * h + b_s[pl.ds(i, 1), :]
        h_s[pl.ds(i, 1), :] = h
        return h

    h = lax.fori_loop(0, tt, step, hcar[...], unroll=8)
    hcar[...] = h
    hl_ref[...] = h
    y_ref[...] = jax.nn.gelu(xg_ref[...]) * h_s[...]
    tail = xbuf[pad + tt - (CONV_W - 1):pad + tt, :]
    nb_ref[...] = tail
    xbuf[pad - (CONV_W - 1):pad, :] = tail


def _lru_weights(lp):
    def blockdiag(w):
        eye = jnp.eye(H_C, dtype=w.dtype)
        return jnp.einsum('hij,hg->higj', w, eye).reshape(D_C, D_C)
    return [lp['conv_w'], lp['conv_b'][None, :], blockdiag(lp['w_a']), lp['b_a'][None, :],
            blockdiag(lp['w_x']), lp['b_x'][None, :], lp['lru_lambda'][None, :]]


def _lru_prompt(xr, xg, conv_buf, h0, lp):
    b, t, _ = xr.shape
    tt = min(TT_LRU, t)
    assert t % tt == 0 and tt >= CONV_W - 1
    weights = _lru_weights(lp)
    row = pl.BlockSpec((None, tt, D_C), lambda i, j: (i, j, 0))
    per_seq = lambda n: pl.BlockSpec((None, n, D_C), lambda i, j: (i, 0, 0))
    full = lambda a: pl.BlockSpec(a.shape, lambda i, j: (0,) * a.ndim)
    y, nb, hl = pl.pallas_call(
        _lru_prompt_kernel,
        grid=(b, t // tt),
        in_specs=[row, row, per_seq(CONV_W - 1), per_seq(1)] + [full(w) for w in weights],
        out_specs=[row, per_seq(CONV_W - 1), per_seq(1)],
        out_shape=[jax.ShapeDtypeStruct((b, t, D_C), F32),
                   jax.ShapeDtypeStruct((b, CONV_W - 1, D_C), F32),
                   jax.ShapeDtypeStruct((b, 1, D_C), F32)],
        scratch_shapes=[pltpu.VMEM((tt + 8, D_C), F32), pltpu.VMEM((tt, D_C), F32),
                        pltpu.VMEM((tt, D_C), F32), pltpu.VMEM((tt, D_C), F32),
                        pltpu.VMEM((1, D_C), F32)],
        compiler_params=_params("parallel", "arbitrary"),
        name="lru_prompt",
    )(xr, xg, conv_buf, h0[:, None, :], *weights)
    return y, nb, hl[:, 0, :]


def _lru_sample_kernel(xr_ref, xg_ref, c0_ref, c1_ref, c2_ref, h0_ref, cw_ref, cbias_ref, wa_ref, ba_ref,
                       wx_ref, bx_ref, lam_ref, y_ref, hl_ref):
    xc = (cbias_ref[...] + c0_ref[...] * cw_ref[0:1, :] + c1_ref[...] * cw_ref[1:2, :]
          + c2_ref[...] * cw_ref[2:3, :] + xr_ref[...] * cw_ref[3:4, :])
    a, bterm = _lru_gates(xc, wa_ref, ba_ref, wx_ref, bx_ref, lam_ref)
    h = a * h0_ref[...] + bterm
    hl_ref[...] = h
    y_ref[...] = jax.nn.gelu(xg_ref[...]) * h


def _lru_sample(xr, xg, conv_buf, h0, lp):
    n = xr.shape[0]
    weights = _lru_weights(lp)
    args = [xr, xg, conv_buf[:, 0], conv_buf[:, 1], conv_buf[:, 2], h0] + weights
    full = lambda a: pl.BlockSpec(a.shape, lambda i: (0,) * a.ndim)
    return pl.pallas_call(
        _lru_sample_kernel,
        grid=(1,),
        in_specs=[full(a) for a in args],
        out_specs=[full(xr), full(xr)],
        out_shape=[jax.ShapeDtypeStruct((n, D_C), F32)] * 2,
        compiler_params=_params("arbitrary"),
        name="lru_sample",
    )(*args)


def _outproj_kernel(*refs, merge):
    if merge:
        o1, o2, o3, l1, l2, l3 = refs[:6]
        refs = refs[6:]
        lses = [l1[...], l2[...], l3[...]]
        mm = jnp.maximum(jnp.maximum(lses[0], lses[1]), lses[2])
        ws = [jnp.exp(l - mm) for l in lses]
        ya = (ws[0] * o1[...] + ws[1] * o2[...] + ws[2] * o3[...]) / (ws[0] + ws[1] + ws[2])
    else:
        ya = refs[0][...]
        refs = refs[1:]
    yb_ref, yc_ref, x_ref, g1_ref, sh_ref, sc_ref, og_ref, w_ref, nf_ref, xm_ref, h2_ref = refs
    og = og_ref[...]
    mix = jnp.concatenate([_rms_rows(ya) * og[:, 0:D_A],
                           _rms_rows(yb_ref[...]) * og[:, D_A:D_A + D_B],
                           _rms_rows(yc_ref[...]) * og[:, D_A + D_B:]], axis=-1)
    x = x_ref[...] + g1_ref[...] * jnp.dot(mix.astype(BF16), w_ref[...], preferred_element_type=F32)
    xm_ref[...] = x
    h2 = _rms_rows(x) * nf_ref[...] * (1.0 + sc_ref[...]) + sh_ref[...]
    h2_ref[...] = h2.astype(BF16)


def _outproj(attn, yb, yc, x, g1, sh2, sc2, lp):
    g, t, _ = x.shape
    tm = min(TM_PROJ, t)
    merge = len(attn) > 1
    r = g1.shape[1]
    row = lambda d: pl.BlockSpec((None, tm, d), lambda b, i: (b, i, 0))
    mod_spec = pl.BlockSpec((None, r, D_MODEL), lambda b, i: (b, 0, 0))
    full = lambda a: pl.BlockSpec(a.shape, lambda b, i: (0,) * a.ndim)
    weights = [lp['out_gain'][None, :], lp['w_out'].astype(BF16), lp['norm_ffn'][None, :]]
    return pl.pallas_call(
        functools.partial(_outproj_kernel, merge=merge),
        grid=(g, t // tm),
        in_specs=[row(D_A)] * len(attn) + [row(D_B), row(D_C), row(D_MODEL), mod_spec, mod_spec, mod_spec]
                 + [full(w) for w in weights],
        out_specs=[row(D_MODEL), row(D_MODEL)],
        out_shape=[jax.ShapeDtypeStruct((g, t, D_MODEL), F32), jax.ShapeDtypeStruct((g, t, D_MODEL), BF16)],
        compiler_params=_params("parallel", "parallel"),
        name="outproj_prompt" if merge else "outproj_sample",
    )(*attn, yb, yc, x, g1, sh2, sc2, *weights)


def _sort16_pairs():
    pairs = []

    def merge(lo, hi, r):
        step = r * 2
        if step < hi - lo:
            merge(lo, hi, step)
            merge(lo + r, hi, step)
            pairs.extend((i, i + r) for i in range(lo + r, hi - r, step))
        else:
            pairs.append((lo, lo + r))

    def sort(lo, hi):
        if hi - lo >= 1:
            mid = lo + (hi - lo) // 2
            sort(lo, mid)
            sort(mid + 1, hi)
            merge(lo, hi, 1)

    sort(0, PEER_TOPK - 1)
    return pairs


_SORT16 = _sort16_pairs()
_BITONIC16 = [(i, i + d) for d in (8, 4, 2, 1) for i in range(PEER_TOPK) if not i & d]


def _exchange(vals, pairs):
    vals = list(vals)
    for i, j in pairs:
        hi, lo = jnp.maximum(vals[i], vals[j]), jnp.minimum(vals[i], vals[j])
        vals[i], vals[j] = hi, lo
    return vals


def _merge_top16(a, b):
    return _exchange([jnp.maximum(a[i], b[PEER_TOPK - 1 - i]) for i in range(PEER_TOPK)], _BITONIC16)


def _across_sublanes(vals):
    for shift in (4, 2, 1):
        vals = _merge_top16(vals, [pltpu.roll(v, shift, axis=0) for v in vals])
    return vals


def _top16_of_keys(s):
    slabs = [s[8 * i:8 * i + 8, :] for i in range(N_KEYS // 8)]
    return _across_sublanes(_exchange(slabs, _SORT16))


def _prefix_count(holds, vals):
    pick = jnp.where
    m8 = holds(vals[7])
    m4 = holds(pick(m8, vals[11], vals[3]))
    m2 = holds(pick(m8, pick(m4, vals[13], vals[9]), pick(m4, vals[5], vals[1])))
    m1 = holds(pick(m8, pick(m4, pick(m2, vals[14], vals[12]), pick(m2, vals[10], vals[8])),
                    pick(m4, pick(m2, vals[6], vals[4]), pick(m2, vals[2], vals[0]))))
    m0 = holds(vals[15])
    return (pick(m8, 8.0, 0.0) + pick(m4, 4.0, 0.0) + pick(m2, 2.0, 0.0) + pick(m1, 1.0, 0.0)
            + pick(m0, 1.0, 0.0))


def _peer_select_kernel(xt_ref, wq_ref, keys_ref, cnt_ref, e1_ref, rk_ref, e2_ref, q_s):
    tn = xt_ref.shape[1]
    sub = lax.broadcasted_iota(jnp.int32, (8, tn), 0)
    q_s[...] = jnp.dot(wq_ref[...], xt_ref[...], preferred_element_type=F32)
    half = D_QUERY // 2

    def head(h, carry):
        s = []
        for p in range(2):
            qt = q_s[pl.ds(pl.multiple_of((2 * h + p) * half, half), half), :]
            qn = qt * lax.rsqrt(jnp.mean(qt * qt, axis=0, keepdims=True) + EPS)
            s.append(_split_dot_rhs(keys_ref[0, 2 * h + p], keys_ref[1, 2 * h + p], qn))
        s1, s2 = s
        v1 = _top16_of_keys(s1)
        v2 = _top16_of_keys(s2)
        a_lo, a_hi = v1[0], v1[8]
        for j in range(1, 8):
            a_lo = jnp.where(sub == j, v1[j], a_lo)
            a_hi = jnp.where(sub == j, v1[8 + j], a_hi)
        sv = _across_sublanes(_merge_top16([a_lo + b for b in v2], [a_hi + b for b in v2]))
        tau = sv[PEER_TOPK - 1][None]
        z = jnp.ones_like(sv[0])
        for kk in range(1, PEER_TOPK):
            z = z + jnp.exp(sv[kk] - sv[0])
        inv_z = 1.0 / z
        for i in range(N_KEYS // 16):
            rows = slice(16 * i, 16 * i + 16)
            s1r = s1[rows].reshape(2, 8, tn)
            s2r = s2[rows].reshape(2, 8, tn)
            cnt = _prefix_count(lambda v: s1r + v >= tau, v2)
            rk = _prefix_count(lambda v: v > s2r, v2)
            cnt_ref[h, rows, :] = cnt.reshape(16, tn)
            e1_ref[h, rows, :] = jnp.exp(s1r - v1[0][None]).reshape(16, tn)
            rk_ref[h, rows, :] = rk.reshape(16, tn).astype(BF16)
            e2_ref[h, rows, :] = (jnp.exp(s2r - v2[0][None]) * inv_z[None]).reshape(16, tn).astype(BF16)
        return carry

    lax.fori_loop(0, PEER_HEADS, head, 0)


def _peer_select(h2t, lp, tn):
    n = h2t.shape[1]
    wq = lp['peer_wq'].T.astype(BF16)
    keys = _bf16_pair(lp['peer_keys'].reshape(2 * PEER_HEADS, N_KEYS, D_QUERY // 2))
    out = pl.BlockSpec((PEER_HEADS, N_KEYS, tn), lambda i: (0, 0, i))
    shape = lambda dt: jax.ShapeDtypeStruct((PEER_HEADS, N_KEYS, n), dt)
    return pl.pallas_call(
        _peer_select_kernel,
        grid=(n // tn,),
        in_specs=[pl.BlockSpec((D_MODEL, tn), lambda i: (0, i)),
                  pl.BlockSpec(wq.shape, lambda i: (0, 0)),
                  pl.BlockSpec(keys.shape, lambda i: (0, 0, 0, 0))],
        out_specs=[out] * 4,
        out_shape=[shape(F32), shape(F32), shape(BF16), shape(BF16)],
        scratch_shapes=[pltpu.VMEM((PEER_HEADS * D_QUERY, tn), F32)],
        compiler_params=_params("parallel"),
        name="peer_select",
    )(h2t, wq, keys)


def _peer_dense_kernel(xt_ref, u0_ref, u_ref, vt_ref, cnt_ref, e1_ref, rk_ref, e2_ref, xm_ref, g2_ref, zbits_ref,
                       o_ref, acc, at_s, at_n, m_s, cntb, e1b):
    j = pl.program_id(1)
    last = pl.num_programs(1) - 1
    te, tn = at_s.shape
    n_r = te // N_KEYS
    slab = DENSE_SLAB

    @pl.when(j == 0)
    def _():
        acc[...] = jnp.zeros_like(acc)
        m_s[...] = jnp.zeros_like(m_s)
        at_s[...] = jnp.dot(u0_ref[...], xt_ref[...], preferred_element_type=F32)

    fed = jnp.dot(vt_ref[...], m_s[...], preferred_element_type=F32)
    acc[...] += fed
    following = jnp.dot(u_ref[...], xt_ref[...], preferred_element_type=F32)
    at_n[...] = following

    def zero_after(x):
        return pltpu.bitcast(pltpu.bitcast(x[-8:, :], jnp.int32) & zbits_ref[...], F32)[0:1, :]

    after_v, after_u = zero_after(fed), zero_after(following)

    r0 = jnp.minimum(j, last - 1) * n_r
    for rr in range(n_r):
        for h in range(PEER_HEADS):
            cntb[rr * PEER_HEADS + h] = jnp.broadcast_to(cnt_ref[h, pl.ds(r0 + rr, 1), :], (slab, tn)).astype(BF16)
            e1b[rr * PEER_HEADS + h] = jnp.broadcast_to(e1_ref[h, pl.ds(r0 + rr, 1), :], (slab, tn)).astype(BF16)
    n_slabs = te // slab
    for s in range(n_slabs):
        rr = s * slab // N_KEYS
        keys2 = slice(s * slab % N_KEYS, s * slab % N_KEYS + slab)
        experts = slice(s * slab, (s + 1) * slab)
        gate = jnp.zeros((slab, tn), BF16)
        for h in range(PEER_HEADS):
            picked = rk_ref[h, keys2, :] < cntb[rr * PEER_HEADS + h]
            gate = gate + jnp.where(picked, e2_ref[h, keys2, :], 0.0) * e1b[rr * PEER_HEADS + h]
        pre = at_s[experts, :]
        if s >= n_slabs - ANCHOR_U_SLABS:
            pre = pre + after_u
        elif s >= n_slabs - ANCHOR_V_SLABS:
            pre = pre + after_v
        m_s[experts, :] = gate * jax.nn.gelu(pre).astype(BF16)

    at_s[...] = at_n[...]

    @pl.when(j == last)
    def _():
        o_ref[...] = xm_ref[...] + g2_ref[...] * acc[...].T


def _peer_dense(h2t, sel, xmid, g2, lp, tn):
    g, t, _ = xmid.shape
    n = g * t
    te = TE_DENSE
    n_exp = lp['peer_u'].shape[0]
    nj = n_exp // te
    u = lp['peer_u'].astype(BF16)
    vt = jnp.transpose(lp['peer_v'].astype(BF16).reshape(nj, te, D_MODEL), (0, 2, 1))
    tiles_per_seq = t // tn
    r = g2.shape[1]
    tok = pl.BlockSpec((PEER_HEADS, N_KEYS, tn), lambda i, j: (0, 0, i))
    row = pl.BlockSpec((None, tn, D_MODEL), lambda i, j: (i // tiles_per_seq, i % tiles_per_seq, 0))
    return pl.pallas_call(
        _peer_dense_kernel,
        grid=(n // tn, nj + 1),
        in_specs=[pl.BlockSpec((D_MODEL, tn), lambda i, j: (0, i)),
                  pl.BlockSpec((te, D_MODEL), lambda i, j: (0, 0)),
                  pl.BlockSpec((te, D_MODEL), lambda i, j: (jnp.minimum(j + 1, nj - 1), 0)),
                  pl.BlockSpec((None, D_MODEL, te), lambda i, j: (jnp.maximum(j - 1, 0), 0, 0)),
                  tok, tok, tok, tok, row,
                  pl.BlockSpec((None, r, D_MODEL), lambda i, j: (i // tiles_per_seq, 0, 0)),
                  pl.BlockSpec((8, tn), lambda i, j: (0, 0))],
        out_specs=row,
        out_shape=jax.ShapeDtypeStruct((g, t, D_MODEL), F32),
        scratch_shapes=[pltpu.VMEM((D_MODEL, tn), F32), pltpu.VMEM((te, tn), F32), pltpu.VMEM((te, tn), F32),
                        pltpu.VMEM((te, tn), BF16),
                        pltpu.VMEM((te // N_KEYS * PEER_HEADS, DENSE_SLAB, tn), BF16),
                        pltpu.VMEM((te // N_KEYS * PEER_HEADS, DENSE_SLAB, tn), BF16)],
        compiler_params=_params("parallel", "arbitrary"),
        name="peer_dense",
    )(h2t, u, u, vt, *sel, xmid, g2, jnp.zeros((8, tn), jnp.int32))


def _peer(xmid, h2, g2, lp, tn_select, tn_dense):
    g, t, _ = xmid.shape
    h2t = h2.reshape(g * t, D_MODEL).T
    sel = _peer_select(h2t, lp, tn_select)
    return _peer_dense(h2t, sel, xmid, g2, lp, tn_dense)


def _layer_prompt(x, mod, lp, biases):
    sh1, sc1, g1, sh2, sc2, g2 = mod
    b, t, _ = x.shape
    q, k, v, yb, gv, xr, xg = _inproj(x, sh1, sc1, lp, chunked=True)
    branches = [_attn_prompt(q, k, v, bias, dil) for bias, (_, dil) in zip(biases, DILATED_PAIRS)]
    attn = [o for o, _ in branches] + [l for _, l in branches]
    yc, new_buf, h_last = _lru_prompt(xr, xg, jnp.zeros((b, CONV_W - 1, D_C), F32), jnp.zeros((b, D_C), F32), lp)
    xmid, h2 = _outproj(attn, yb, yc, x, g1, sh2, sc2, lp)
    x = _peer(xmid, h2, g2, lp, TN_SELECT, TN_DENSE)
    keep = min(W_MAX, t)
    state = (k[:, t - keep:].reshape(b, keep, H_A, HEAD_DIM), v[:, t - keep:].reshape(b, keep, H_A, HEAD_DIM),
             gv[:, t - CHUNK:], new_buf, h_last)
    return x, state


def _layer_sample(x, mod, lp, rel_bias, cache_k, cache_v, conv_buf, h0, layer):
    sh1, sc1, g1, sh2, sc2, g2 = mod
    _, n, _ = x.shape
    q, k, v, yb, gv, xr, xg = _inproj(x, sh1, sc1, lp, chunked=False)
    ya = _attn_sample(q[0], k[0], v[0], cache_k, cache_v, layer, rel_bias)
    yc, h_last = _lru_sample(xr[0], xg[0], conv_buf, h0, lp)
    xmid, h2 = _outproj([ya[None]], yb, yc[None], x, g1, sh2, sc2, lp)
    x = _peer(xmid, h2, g2, lp, n, n)
    new_buf = jnp.concatenate([conv_buf[:, 1:], xr[0][:, None, :]], axis=1)
    state = (k[0].reshape(n, 1, H_A, HEAD_DIM), v[0].reshape(n, 1, H_A, HEAD_DIM), gv[0][:, None, :],
             new_buf, h_last)
    return x, state


def kernel(x_prompt, x_sample, cache_k, cache_v, state_conv, state_h, c_prompt, c_sample, rel_bias, w_ada, b_ada, norm_mix, norm_ffn, w_in, q_gain, k_gain, gmlp_norm, w_s, b_s, conv_w, conv_b, w_a, b_a, w_x, b_x, lru_lambda, out_gain, w_out, peer_wq, peer_keys, peer_u, peer_v):
    depth = w_in.shape[0]
    nb, ns = x_prompt.shape[0], x_sample.shape[0]
    assert x_sample.shape[1] == 1
    pad = -(nb + ns) % 8
    c_all = jnp.concatenate([c_prompt, c_sample, jnp.zeros((pad, D_MODEL), F32)], axis=0)
    mods = _ada(c_all, w_ada, b_ada)
    biases = [_prompt_bias(rel_bias, window, dil) for window, dil in DILATED_PAIRS]
    xp = x_prompt
    xs = x_sample.reshape(1, ns, D_MODEL)
    st_p, st_s = [], []
    for l in range(depth):
        lp = dict(norm_mix=norm_mix[l], norm_ffn=norm_ffn[l], w_in=w_in[l], q_gain=q_gain[l], k_gain=k_gain[l],
                  gmlp_norm=gmlp_norm[l], w_s=w_s[l], b_s=b_s[l], conv_w=conv_w[l], conv_b=conv_b[l],
                  w_a=w_a[l], b_a=b_a[l], w_x=w_x[l], b_x=b_x[l], lru_lambda=lru_lambda[l],
                  out_gain=out_gain[l], w_out=w_out[l], peer_wq=peer_wq[l], peer_keys=peer_keys[l],
                  peer_u=peer_u[l], peer_v=peer_v[l])
        mod_p = [m[:, None, :] for m in jnp.split(mods[l, :nb], 6, axis=-1)]
        mod_s = [m[None] for m in jnp.split(mods[l, nb:nb + ns], 6, axis=-1)]
        xp, sp = _layer_prompt(xp, mod_p, lp, biases)
        xs, ss = _layer_sample(xs, mod_s, lp, rel_bias, cache_k, cache_v, state_conv[l], state_h[l], l)
        st_p.append(sp)
        st_s.append(ss)
    stack = lambda sts, i: jnp.stack([s[i] for s in sts])
    return (xp, xs.reshape(ns, 1, D_MODEL),
            stack(st_p, 0), stack(st_p, 1), stack(st_s, 0), stack(st_s, 1),
            stack(st_p, 2), stack(st_s, 2), stack(st_p, 3), stack(st_s, 3),
            stack(st_p, 4), stack(st_s, 4))
```
